```python
import math
import jax, jax.numpy as jnp
from jax import lax
import numpy as np

D_MODEL = 1024
BATCH = 8
SEQ = 2048
DEPTH = 4
DEC_BATCH = 32
DEC_SEQ = 4
PAST_LEN = 8192
PAGE_SIZE = 128

N_AB_LAYERS = (DEPTH + 1) // 2
N_C_LAYERS = DEPTH // 2
HG_HEADS = 4
HG_DK = D_MODEL // 8
HG_DV = D_MODEL // 8
HG_WIDTH = HG_HEADS * HG_DK
HG_CHUNK = 64
LB_FLOOR = 1e-30
S5_WIDTH = D_MODEL // 2
S5_GROUP = 16
S5_GROUPS = S5_WIDTH // S5_GROUP
S5_STATE = 64
AB_IN = 4 * HG_WIDTH + S5_WIDTH
AB_MIX = HG_WIDTH + S5_WIDTH
DA_HEADS = 8
DA_DK = D_MODEL // (2 * DA_HEADS)
DA_DV = 2 * DA_DK
DA_QK = DA_HEADS * 2 * DA_DK
C_IN = 2 * DA_QK + DA_HEADS * DA_DV
C_MIX = DA_HEADS * DA_DV
DA_Q_BLOCK = 128
NEG_BIG = -1e30
MOE_GROUPS = 4
MOE_PER_GROUP = 4
MOE_EXPERTS = MOE_GROUPS * MOE_PER_GROUP
MOE_TOPK = 2
MOE_FF = D_MODEL // 2
DN_ALPHA = (2.0 * DEPTH) ** 0.25
DN_BETA = (8.0 * DEPTH) ** -0.25
LN_EPS = 1e-5

kernel_name = "hgrn2_s5_diffattn_hmoe_step"


def layer_norm(x, g, b):
    xf = x.astype(jnp.float32)
    mu = jnp.mean(xf, axis=-1, keepdims=True)
    var = jnp.mean(jnp.square(xf - mu), axis=-1, keepdims=True)
    y = (xf - mu) * lax.rsqrt(var + LN_EPS) * g.astype(jnp.float32) + b.astype(jnp.float32)
    return y.astype(x.dtype)


def rms_norm(x, g):
    xf = x.astype(jnp.float32)
    y = xf * lax.rsqrt(jnp.mean(jnp.square(xf), axis=-1, keepdims=True) + LN_EPS) * g.astype(jnp.float32)
    return y.astype(x.dtype)


def hgrn2_recurrence(q, logf, k, v, s0):
    bt, t, h, dk = q.shape
    L = math.gcd(t, HG_CHUNK)
    nc = t // L

    def to_chunks(a):
        return a.reshape(bt, nc, L, h, a.shape[-1]).transpose(1, 0, 3, 2, 4)

    causal = jnp.tril(jnp.ones((L, L), dtype=bool))

    def step(S, inp):
        qi, fi, ki, vi = inp
        b = jnp.cumsum(fi, axis=2)
        o_inter = jnp.einsum('bhlk,bhkv->bhlv', qi * jnp.exp(b), S)
        rel = b[:, :, :, None, :] - b[:, :, None, :, :]
        decay = jnp.exp(jnp.where(causal[:, :, None], rel, NEG_BIG))
        att = jnp.einsum('bhik,bhijk,bhjk->bhij', qi, decay, ki)
        o = o_inter + jnp.einsum('bhij,bhjv->bhiv', att, vi)
        b_last = b[:, :, -1:, :]
        S_new = jnp.exp(b_last[:, :, 0, :])[..., None] * S + jnp.einsum(
            'bhjk,bhjv->bhkv', ki * jnp.exp(b_last - b), vi)
        return S_new, o

    s_final, oc = lax.scan(step, s0.astype(jnp.float32),
                           (to_chunks(q), to_chunks(logf), to_chunks(k), to_chunks(v)))
    o = oc.transpose(1, 0, 3, 2, 4).reshape(bt, t, h, -1)
    return o, s_final


def hgrn2_branch(zq, zf, zi, zg, lb, norm_g, s0):
    bt, t, _ = zq.shape
    f32 = jnp.float32
    log_lb = jnp.log(jnp.maximum(lb, LB_FLOOR))
    logf = jnp.logaddexp(log_lb, jnp.log1p(-lb) + jax.nn.log_sigmoid(zf.astype(f32)))
    k = -jnp.expm1(logf)
    q = jax.nn.silu(zq.astype(f32))
    heads = lambda a: a.reshape(bt, t, HG_HEADS, -1)
    o, s_new = hgrn2_recurrence(heads(q), heads(logf), heads(k), heads(zi.astype(f32)), s0)
    o = rms_norm(o, norm_g.reshape(HG_HEADS, HG_DV)) * jax.nn.silu(heads(zg.astype(f32)))
    return o.reshape(bt, t, HG_WIDTH), s_new


def s5_branch(u, lam_re, lam_im, log_step, b_re, b_im, c_re, c_im, d, w_glu, h0_re, h0_im):
    f32 = jnp.float32
    bt, t, _ = u.shape
    uf = u.astype(f32)
    ug = uf.reshape(bt, t, S5_GROUPS, S5_GROUP)
    lr, li = lam_re.astype(f32), lam_im.astype(f32)
    dt = jnp.exp(log_step.astype(f32))[:, None]
    mag = jnp.exp(lr * dt)
    ab_re, ab_im = mag * jnp.cos(li * dt), mag * jnp.sin(li * dt)
    den = lr * lr + li * li
    nr, ni = ab_re - 1.0, ab_im
    coef_re = (nr * lr + ni * li) / den
    coef_im = (ni * lr - nr * li) / den
    br, bi = b_re.astype(f32), b_im.astype(f32)
    bb_re = coef_re[..., None] * br - coef_im[..., None] * bi
    bb_im = coef_re[..., None] * bi + coef_im[..., None] * br
    x_re = jnp.einsum('btgc,gpc->btgp', ug, bb_re)
    x_im = jnp.einsum('btgc,gpc->btgp', ug, bb_im)
    h0r, h0i = h0_re.astype(f32), h0_im.astype(f32)
    x_re = x_re.at[:, 0].add(ab_re * h0r - ab_im * h0i)
    x_im = x_im.at[:, 0].add(ab_re * h0i + ab_im * h0r)
    a_re = jnp.broadcast_to(ab_re, x_re.shape)
    a_im = jnp.broadcast_to(ab_im, x_im.shape)

    def combine(e1, e2):
        a1r, a1i, b1r, b1i = e1
        a2r, a2i, b2r, b2i = e2
        return (a1r * a2r - a1i * a2i, a1r * a2i + a1i * a2r,
                a2r * b1r - a2i * b1i + b2r, a2r * b1i + a2i * b1r + b2i)

    _, _, h_re, h_im = lax.associative_scan(combine, (a_re, a_im, x_re, x_im), axis=1)
    y = (jnp.einsum('btgp,gcp->btgc', h_re, c_re.astype(f32))
         - jnp.einsum('btgp,gcp->btgc', h_im, c_im.astype(f32)))
    y = y.reshape(bt, t, S5_WIDTH) + d.astype(f32) * uf
    y = jax.nn.gelu(y)
    a, gate = jnp.split(y @ w_glu.astype(f32), 2, axis=-1)
    return a * jax.nn.sigmoid(gate), h_re[:, -1], h_im[:, -1]


def diff_attn_core(q, q_pos, k_segs, v_segs, kpos_segs, lam, slopes):
    f32 = jnp.float32
    qf = q.astype(f32) * (DA_DK ** -0.5)
    s_list = []
    for k, kp in zip(k_segs, kpos_segs):
        s = jnp.einsum('bqhcd,bkhcd->bhcqk', qf, k.astype(f32))
        dist = (q_pos[:, None] - kp[None, :]).astype(f32)
        bias = jnp.where(dist >= 0, -slopes[:, None, None] * dist, NEG_BIG)
        s_list.append(s + bias[None, :, None])
    p = jax.nn.softmax(jnp.concatenate(s_list, axis=-1), axis=-1)
    w = p[:, :, 0] - lam * p[:, :, 1]
    outs = []
    off = 0
    for v, kp in zip(v_segs, kpos_segs):
        n = kp.shape[0]
        outs.append(jnp.einsum('bhqk,bkhv->bqhv', w[..., off:off + n], v.astype(f32)))
        off += n
    return sum(outs[1:], outs[0])


def prompt_attention(q, k, v, lam, slopes):
    bt, t = q.shape[:2]
    qb_len = math.gcd(t, DA_Q_BLOCK)
    nb = t // qb_len
    qb = q.reshape(bt, nb, qb_len, DA_HEADS, 2, DA_DK).transpose(1, 0, 2, 3, 4, 5)
    pos = jnp.arange(t)
    posb = pos.reshape(nb, qb_len)
    out = lax.map(lambda a: diff_attn_core(a[0], a[1], (k,), (v,), (pos,), lam, slopes), (qb, posb))
    return out.transpose(1, 0, 2, 3, 4).reshape(bt, t, DA_HEADS, DA_DV)


def sample_attention(q, k, v, lam, slopes, cache_k, cache_v, page_table, layer_c):
    db, n_pages = page_table.shape
    past = n_pages * PAGE_SIZE
    kp = cache_k[layer_c, page_table].reshape(db, past, DA_HEADS, 2, DA_DK)
    vp = cache_v[layer_c, page_table].reshape(db, past, DA_HEADS, DA_DV)
    new_pos = past + jnp.arange(q.shape[1])
    return diff_attn_core(q, new_pos, (kp, k), (vp, v), (jnp.arange(past), new_pos), lam, slopes)


def diff_attn_branch(z, lq1, lk1, lq2, lk2, subln_g, layer_idx, slopes, paged, layer_c):
    f32 = jnp.float32
    bt, t, _ = z.shape
    zq, zk, zv = jnp.split(z, [DA_QK, 2 * DA_QK], axis=-1)
    q = zq.reshape(bt, t, DA_HEADS, 2, DA_DK)
    k = zk.reshape(bt, t, DA_HEADS, 2, DA_DK)
    v = zv.reshape(bt, t, DA_HEADS, DA_DV)
    lam_init = 0.8 - 0.6 * math.exp(-0.3 * layer_idx)
    lam = (jnp.exp(jnp.sum(lq1.astype(f32) * lk1.astype(f32)))
           - jnp.exp(jnp.sum(lq2.astype(f32) * lk2.astype(f32))) + lam_init)
    if paged is None:
        o = prompt_attention(q, k, v, lam, slopes)
    else:
        cache_k, cache_v, page_table = paged
        o = sample_attention(q, k, v, lam, slopes, cache_k, cache_v, page_table, layer_c)
    o = rms_norm(o, subln_g) * (1.0 - lam_init)
    return o.reshape(bt, t, C_MIX), k.reshape(bt, t, DA_HEADS, 2 * DA_DK), v


def hier_moe(x, w_grp, b_grp, w_exp, b_exp, w_gate, w_up, w_down):
    f32 = jnp.float32
    shp = x.shape
    xf = x.reshape(-1, shp[-1])
    n = xf.shape[0]
    rows = jnp.arange(n)
    g_logits = (xf @ w_grp + b_grp).astype(f32)
    g_prob = jax.nn.softmax(g_logits, axis=-1)
    g_idx = jnp.argmax(g_logits, axis=-1)
    p_g = g_prob[rows, g_idx][:, None]
    e_logits = (xf @ w_exp + b_exp).astype(f32).reshape(n, MOE_GROUPS, MOE_PER_GROUP)
    e_in = e_logits[rows, g_idx]
    top_v, top_i = lax.top_k(e_in, MOE_TOPK)
    top_w = jax.nn.softmax(top_v, axis=-1) * p_g
    flat = g_idx[:, None] * MOE_PER_GROUP + top_i
    gates = jnp.sum(jax.nn.one_hot(flat, MOE_EXPERTS, dtype=f32) * top_w[..., None], axis=1)
    h = jax.nn.silu(jnp.einsum('nd,edf->nef', xf, w_gate)) * jnp.einsum('nd,edf->nef', xf, w_up)
    h = h * gates.astype(h.dtype)[:, :, None]
    y = jnp.einsum('nef,efd->nd', h, w_down)
    return y.reshape(shp).astype(x.dtype)


def trunk(x, w, s_hg, s_re, s_im, paged):
    f32 = jnp.float32
    bt, t, _ = x.shape
    slopes = 2.0 ** (-8.0 * jnp.arange(1, DA_HEADS + 1, dtype=f32) / DA_HEADS)
    sm = jax.nn.softmax(w['hg_lb_param'].astype(f32), axis=0)
    lbs = jnp.cumsum(sm, axis=0) - sm[:1]
    new_k, new_v, new_hg, new_re, new_im = [], [], [], [], []
    for i in range(DEPTH):
        j = i // 2
        if i % 2 == 0:
            z = x @ w['w_in_ab'][j]
            zq, zf, zi, zg, zu = jnp.split(z, [HG_WIDTH, 2 * HG_WIDTH, 3 * HG_WIDTH, 4 * HG_WIDTH], axis=-1)
            o_a, s_a = hgrn2_branch(zq, zf, zi, zg, lbs[j], w['hg_norm_g'][j], s_hg[j])
            o_b, h_re, h_im = s5_branch(zu, w['s5_lam_re'][j], w['s5_lam_im'][j], w['s5_log_step'][j],
                                        w['s5_b_re'][j], w['s5_b_im'][j], w['s5_c_re'][j], w['s5_c_im'][j],
                                        w['s5_d'][j], w['s5_w_glu'][j], s_re[j], s_im[j])
            mix = jnp.concatenate([o_a.astype(x.dtype), o_b.astype(x.dtype)], axis=-1) @ w['w_out_ab'][j]
            new_hg.append(s_a)
            new_re.append(h_re)
            new_im.append(h_im)
        else:
            z = x @ w['w_in_c'][j]
            o_c, k_rows, v_rows = diff_attn_branch(z, w['da_lam_q1'][j], w['da_lam_k1'][j], w['da_lam_q2'][j],
                                                   w['da_lam_k2'][j], w['da_subln_g'][j], i, slopes, paged, j)
            mix = o_c.astype(x.dtype) @ w['w_out_c'][j]
            new_k.append(k_rows)
            new_v.append(v_rows)
        x = layer_norm(DN_ALPHA * x + mix, w['ln_g'][i, 0], w['ln_b'][i, 0])
        ffn = hier_moe(x, w['moe_w_group'][i], w['moe_b_group'][i], w['moe_w_expert'][i], w['moe_b_expert'][i],
                       w['moe_w_gate'][i], w['moe_w_up'][i], w['moe_w_down'][i])
        x = layer_norm(DN_ALPHA * x + ffn, w['ln_g'][i, 1], w['ln_b'][i, 1])
    return x, jnp.stack(new_k), jnp.stack(new_v), jnp.stack(new_hg), jnp.stack(new_re), jnp.stack(new_im)


def setup_inputs(seed: int = 0) -> dict:
    key = jax.random.key(seed)
    ks = iter(jax.random.split(key, 48))
    f32 = jnp.float32
    nrm = lambda shape, scale: jax.random.normal(next(ks), shape, f32) * scale
    n_pages = PAST_LEN // PAGE_SIZE
    n_pool = (DEC_BATCH * n_pages * 5) // 4
    inp = {}
    inp['x_prompt'] = nrm((BATCH, SEQ, D_MODEL), 1.0)
    inp['x_sample'] = nrm((DEC_BATCH, DEC_SEQ, D_MODEL), 1.0)
    inp['cache_k'] = nrm((N_C_LAYERS, n_pool, PAGE_SIZE, DA_HEADS, 2 * DA_DK), 1.0)
    inp['cache_v'] = nrm((N_C_LAYERS, n_pool, PAGE_SIZE, DA_HEADS, DA_DV), 1.0)
    inp['state_hgrn'] = nrm((N_AB_LAYERS, DEC_BATCH, HG_HEADS, HG_DK, HG_DV), 0.3)
    inp['state_s5_re'] = nrm((N_AB_LAYERS, DEC_BATCH, S5_GROUPS, S5_STATE), 0.5)
    inp['state_s5_im'] = nrm((N_AB_LAYERS, DEC_BATCH, S5_GROUPS, S5_STATE), 0.5)
    perm = jax.random.permutation(next(ks), n_pool)
    inp['page_table'] = perm[:DEC_BATCH * n_pages].reshape(DEC_BATCH, n_pages).astype(jnp.int32)
    inp['w_in_ab'] = nrm((N_AB_LAYERS, D_MODEL, AB_IN), D_MODEL ** -0.5)
    inp['w_out_ab'] = nrm((N_AB_LAYERS, AB_MIX, D_MODEL), DN_BETA * AB_MIX ** -0.5)
    inp['hg_lb_param'] = nrm((N_AB_LAYERS, HG_WIDTH), 0.5)
    inp['hg_norm_g'] = 1.0 + nrm((N_AB_LAYERS, HG_WIDTH), 0.02)
    inp['s5_lam_re'] = -0.5 + nrm((N_AB_LAYERS, S5_GROUPS, S5_STATE), 0.01)
    inp['s5_lam_im'] = (math.pi * jnp.arange(S5_STATE, dtype=f32))[None, None, :] + nrm(
        (N_AB_LAYERS, S5_GROUPS, S5_STATE), 0.01)
    inp['s5_log_step'] = jax.random.uniform(next(ks), (N_AB_LAYERS, S5_GROUPS), f32,
                                            minval=math.log(1e-3), maxval=math.log(1e-1))
    inp['s5_b_re'] = nrm((N_AB_LAYERS, S5_GROUPS, S5_STATE, S5_GROUP), (2 * S5_GROUP) ** -0.5)
    inp['s5_b_im'] = nrm((N_AB_LAYERS, S5_GROUPS, S5_STATE, S5_GROUP), (2 * S5_GROUP) ** -0.5)
    inp['s5_c_re'] = nrm((N_AB_LAYERS, S5_GROUPS, S5_GROUP, S5_STATE), (2 * S5_STATE) ** -0.5)
    inp['s5_c_im'] = nrm((N_AB_LAYERS, S5_GROUPS, S5_GROUP, S5_STATE), (2 * S5_STATE) ** -0.5)
    inp['s5_d'] = nrm((N_AB_LAYERS, S5_WIDTH), 1.0)
    inp['s5_w_glu'] = nrm((N_AB_LAYERS, S5_WIDTH, 2 * S5_WIDTH), S5_WIDTH ** -0.5)
    inp['w_in_c'] = nrm((N_C_LAYERS, D_MODEL, C_IN), D_MODEL ** -0.5)
    inp['w_out_c'] = nrm((N_C_LAYERS, C_MIX, D_MODEL), DN_BETA * C_MIX ** -0.5)
    inp['da_lam_q1'] = nrm((N_C_LAYERS, DA_DK), 0.1)
    inp['da_lam_k1'] = nrm((N_C_LAYERS, DA_DK), 0.1)
    inp['da_lam_q2'] = nrm((N_C_LAYERS, DA_DK), 0.1)
    inp['da_lam_k2'] = nrm((N_C_LAYERS, DA_DK), 0.1)
    inp['da_subln_g'] = 1.0 + nrm((N_C_LAYERS, DA_DV), 0.02)
    inp['moe_w_group'] = nrm((DEPTH, D_MODEL, MOE_GROUPS), D_MODEL ** -0.5)
    inp['moe_b_group'] = nrm((DEPTH, MOE_GROUPS), 0.01)
    inp['moe_w_expert'] = nrm((DEPTH, D_MODEL, MOE_EXPERTS), D_MODEL ** -0.5)
    inp['moe_b_expert'] = nrm((DEPTH, MOE_EXPERTS), 0.01)
    inp['moe_w_gate'] = nrm((DEPTH, MOE_EXPERTS, D_MODEL, MOE_FF), D_MODEL ** -0.5)
    inp['moe_w_up'] = nrm((DEPTH, MOE_EXPERTS, D_MODEL, MOE_FF), D_MODEL ** -0.5)
    inp['moe_w_down'] = nrm((DEPTH, MOE_EXPERTS, MOE_FF, D_MODEL), DN_BETA * MOE_FF ** -0.5)
    inp['ln_g'] = 1.0 + nrm((DEPTH, 2, D_MODEL), 0.02)
    inp['ln_b'] = nrm((DEPTH, 2, D_MODEL), 0.02)
    return inp


def reference(x_prompt, x_sample, cache_k, cache_v, state_hgrn, state_s5_re, state_s5_im, page_table,
              w_in_ab, w_out_ab, hg_lb_param, hg_norm_g, s5_lam_re, s5_lam_im, s5_log_step,
              s5_b_re, s5_b_im, s5_c_re, s5_c_im, s5_d, s5_w_glu,
              w_in_c, w_out_c, da_lam_q1, da_lam_k1, da_lam_q2, da_lam_k2, da_subln_g,
              moe_w_group, moe_b_group, moe_w_expert, moe_b_expert, moe_w_gate, moe_w_up, moe_w_down,
              ln_g, ln_b):
    w = dict(w_in_ab=w_in_ab, w_out_ab=w_out_ab, hg_lb_param=hg_lb_param, hg_norm_g=hg_norm_g,
             s5_lam_re=s5_lam_re, s5_lam_im=s5_lam_im, s5_log_step=s5_log_step,
             s5_b_re=s5_b_re, s5_b_im=s5_b_im, s5_c_re=s5_c_re, s5_c_im=s5_c_im, s5_d=s5_d, s5_w_glu=s5_w_glu,
             w_in_c=w_in_c, w_out_c=w_out_c, da_lam_q1=da_lam_q1, da_lam_k1=da_lam_k1,
             da_lam_q2=da_lam_q2, da_lam_k2=da_lam_k2, da_subln_g=da_subln_g,
             moe_w_group=moe_w_group, moe_b_group=moe_b_group, moe_w_expert=moe_w_expert,
             moe_b_expert=moe_b_expert, moe_w_gate=moe_w_gate, moe_w_up=moe_w_up, moe_w_down=moe_w_down,
             ln_g=ln_g, ln_b=ln_b)
    f32 = jnp.float32
    bp = x_prompt.shape[0]
    zeros_hg = jnp.zeros((N_AB_LAYERS, bp, HG_HEADS, HG_DK, HG_DV), f32)
    zeros_s5 = jnp.zeros((N_AB_LAYERS, bp, S5_GROUPS, S5_STATE), f32)
    y_p, k_p, v_p, hg_p, re_p, im_p = trunk(x_prompt, w, zeros_hg, zeros_s5, zeros_s5, None)
    y_s, k_s, v_s, hg_s, re_s, im_s = trunk(x_sample, w, state_hgrn, state_s5_re, state_s5_im,
                                            (cache_k, cache_v, page_table))
    return (y_p, y_s, k_p, v_p, hg_p, re_p, im_p, k_s, v_s, hg_s, re_s, im_s)
```

```python
import functools
import math

import jax
import jax.numpy as jnp
from jax import lax
from jax.experimental import pallas as pl
from jax.experimental.pallas import tpu as pltpu

F32 = jnp.float32
BF16 = jnp.bfloat16

D_MODEL = 1024
DEPTH = 4
HG_HEADS = 4
HG_DK = 128
HG_WIDTH = HG_HEADS * HG_DK
S5_WIDTH = 512
S5_GROUPS = 32
S5_GROUP = 16
S5_STATE = 64
S5_N = S5_GROUPS * S5_STATE
DA_HEADS = 8
DA_DK = 64
DA_DV = 128
DA_QK = DA_HEADS * 2 * DA_DK
MOE_GROUPS = 4
MOE_PER_GROUP = 4
MOE_EXPERTS = 16
MOE_FF = 512
PAGE_SIZE = 128
DN_ALPHA = (2.0 * DEPTH) ** 0.25
LN_EPS = 1e-5
NEG_BIG = -1e30
LB_FLOOR = 1e-30

SUBLANES = 8
LANES = 128
VMEM_LIMIT = 48 * 1024 * 1024

HG_CHUNK = 64
ROUTE_OFF = MOE_GROUPS


def _cparams(*sem):
    return pltpu.CompilerParams(dimension_semantics=sem, vmem_limit_bytes=VMEM_LIMIT)


def _dot(a, b):
    return jnp.dot(a, b, preferred_element_type=F32)


def _dot_nt(a, b):
    return lax.dot_general(a, b, (((1,), (1,)), ((), ())), preferred_element_type=F32)


def _dot_tn(a, b):
    return lax.dot_general(a, b, (((0,), (0,)), ((), ())), preferred_element_type=F32)


def _sigmoid(x):
    return 1.0 / (1.0 + jnp.exp(-x))


def _silu(x):
    return x * _sigmoid(x)


def _expm1(x):
    u = jnp.exp(x)
    um1 = u - 1.0
    r = jnp.where(u == 1.0, x, um1 * x / jnp.log(u))
    return jnp.where(um1 == -1.0, -1.0, r)


def _layer_norm(y, g, b):
    mu = jnp.mean(y, axis=-1, keepdims=True)
    yc = y - mu
    var = jnp.mean(yc * yc, axis=-1, keepdims=True)
    return yc * lax.rsqrt(var + LN_EPS) * g + b


def _mm_kernel(x_ref, w_ref, o_ref):
    o_ref[...] = _dot(x_ref[...].astype(BF16), w_ref[...])


def _matmul(x, w, tm):
    m, k = x.shape
    n = w.shape[1]
    tm = min(tm, m)
    return pl.pallas_call(
        _mm_kernel,
        grid=(m // tm,),
        in_specs=[pl.BlockSpec((tm, k), lambda i: (i, 0)),
                  pl.BlockSpec((k, n), lambda i: (0, 0))],
        out_specs=pl.BlockSpec((tm, n), lambda i: (i, 0)),
        out_shape=jax.ShapeDtypeStruct((m, n), F32),
        compiler_params=_cparams("arbitrary"),
    )(x, w)


def _proj_ln_kernel(*refs, n_lhs):
    a_refs = refs[:n_lhs]
    w_refs = refs[n_lhs:2 * n_lhs]
    x_ref, g_ref, b_ref, o_ref = refs[2 * n_lhs:]
    acc = _dot(a_refs[0][...].astype(BF16), w_refs[0][...])
    for a_ref, w_ref in zip(a_refs[1:], w_refs[1:]):
        acc = acc + _dot(a_ref[...].astype(BF16), w_ref[...])
    o_ref[...] = _layer_norm(DN_ALPHA * x_ref[...] + acc, g_ref[...], b_ref[...])


def _proj_ln(lhs, ws, x, g, b, tm):
    m, d = x.shape
    tm = min(tm, m)
    n_lhs = len(lhs)
    in_specs = [pl.BlockSpec((tm, a.shape[1]), lambda i: (i, 0)) for a in lhs]
    in_specs += [pl.BlockSpec(w.shape, lambda i: (0, 0)) for w in ws]
    in_specs += [pl.BlockSpec((tm, d), lambda i: (i, 0)),
                 pl.BlockSpec((1, d), lambda i: (0, 0)),
                 pl.BlockSpec((1, d), lambda i: (0, 0))]
    return pl.pallas_call(
        functools.partial(_proj_ln_kernel, n_lhs=n_lhs),
        grid=(m // tm,),
        in_specs=in_specs,
        out_specs=pl.BlockSpec((tm, d), lambda i: (i, 0)),
        out_shape=jax.ShapeDtypeStruct((m, d), F32),
        compiler_params=_cparams("arbitrary"),
    )(*lhs, *ws, x, g.reshape(1, d), b.reshape(1, d))


def _cumsum_rows(x, row):
    n = x.shape[0]
    s = 1
    while s < n:
        x = x + jnp.where(row >= s, pltpu.roll(x, s, axis=0), 0.0)
        s *= 2
    return x


def _hgrn_kernel(zq_ref, zf_ref, zi_ref, zg_ref, lb_ref, ng_ref, s0_ref,
                 o_ref, st_ref, st_scr, *, chunk, t_valid, n_chunks):
    c = pl.program_id(1)
    nb = chunk // SUBLANES

    @pl.when(c == 0)
    def _():
        st_scr[...] = s0_ref[0]

    row = lax.broadcasted_iota(jnp.int32, (chunk, HG_DK), 0)
    row8 = lax.broadcasted_iota(jnp.int32, (SUBLANES, HG_DK), 0)
    rowa = lax.broadcasted_iota(jnp.int32, (SUBLANES, chunk), 0)
    cola = lax.broadcasted_iota(jnp.int32, (SUBLANES, chunk), 1)

    for h in range(HG_HEADS):
        sl = slice(h * HG_DK, (h + 1) * HG_DK)
        zf = zf_ref[:, sl]
        zq = zq_ref[:, sl]
        zg = zg_ref[:, sl]
        v = zi_ref[:, sl]
        lb = lb_ref[:, sl]
        log_lb = jnp.log(jnp.maximum(lb, LB_FLOOR))
        log_sig = jnp.minimum(zf, 0.0) - jnp.log1p(jnp.exp(-jnp.abs(zf)))
        t = jnp.log1p(-lb) + log_sig
        logf = jnp.maximum(log_lb, t) + jnp.log1p(jnp.exp(-jnp.abs(log_lb - t)))
        k = -_expm1(logf)
        q = _silu(zq)
        if t_valid < chunk:
            valid = row < t_valid
            logf = jnp.where(valid, logf, 0.0)
            k = jnp.where(valid, k, 0.0)
        b = _cumsum_rows(logf, row)
        b_last = b[chunk - 1:chunk]
        st = st_scr[h]
        vb = v.astype(BF16)

        o = _dot_nt((q * jnp.exp(b)).astype(BF16), st.astype(BF16))
        blocks = []
        for i in range(nb):
            rs = slice(i * SUBLANES, (i + 1) * SUBLANES)
            qi, ki, bi = q[rs], k[rs], b[rs]
            if i > 0:
                r = b[i * SUBLANES - 1:i * SUBLANES]
                qt = qi * jnp.exp(bi - r)
                kt = k * jnp.exp(jnp.minimum(r - b, 0.0))
                att = _dot_nt(qt.astype(BF16), kt.astype(BF16))
                att = jnp.where(cola < i * SUBLANES, att, 0.0)
            else:
                att = jnp.zeros((SUBLANES, chunk), F32)
            for d in range(SUBLANES):
                if d == 0:
                    prod = qi * ki
                else:
                    kd = pltpu.roll(ki, d, axis=0)
                    bd = pltpu.roll(bi, d, axis=0)
                    dec = jnp.exp(jnp.where(row8 >= d, bi - bd, NEG_BIG))
                    prod = qi * kd * dec
                a = jnp.sum(prod, axis=1, keepdims=True)
                att = att + jnp.where(cola == rowa + (i * SUBLANES - d), a, 0.0)
            blocks.append(att)
        att = blocks[0] if nb == 1 else jnp.concatenate(blocks, axis=0)
        o = o + _dot(att.astype(BF16), vb)

        ke = k * jnp.exp(b_last - b)
        st_scr[h] = jnp.exp(b_last) * st + _dot_tn(vb, ke.astype(BF16))

        on = o * lax.rsqrt(jnp.mean(o * o, axis=-1, keepdims=True) + LN_EPS) * ng_ref[:, sl]
        o_ref[:, sl] = on * _silu(zg)

    @pl.when(c == n_chunks - 1)
    def _():
        st_ref[0] = st_scr[...]


def _hgrn(z, lb, norm_g, s0t, n_seq, t_len, chunk, t_valid):
    n_chunks = t_len // chunk
    zspec = lambda cb: pl.BlockSpec((chunk, HG_WIDTH), lambda b, c: (b * n_chunks + c, cb))
    pspec = pl.BlockSpec((1, HG_WIDTH), lambda b, c: (0, 0))
    sspec = pl.BlockSpec((1, HG_HEADS, HG_DK, HG_DK), lambda b, c: (b, 0, 0, 0))
    return pl.pallas_call(
        functools.partial(_hgrn_kernel, chunk=chunk, t_valid=t_valid, n_chunks=n_chunks),
        grid=(n_seq, n_chunks),
        in_specs=[zspec(0), zspec(1), zspec(2), zspec(3), pspec, pspec, sspec],
        out_specs=[pl.BlockSpec((chunk, HG_WIDTH), lambda b, c: (b * n_chunks + c, 0)), sspec],
        out_shape=[jax.ShapeDtypeStruct((n_seq * t_len, HG_WIDTH), F32),
                   jax.ShapeDtypeStruct(s0t.shape, F32)],
        scratch_shapes=[pltpu.VMEM((HG_HEADS, HG_DK, HG_DK), F32)],
        compiler_params=_cparams("arbitrary", "arbitrary"),
    )(z, z, z, z, lb.reshape(1, HG_WIDTH), norm_g.reshape(1, HG_WIDTH), s0t)


S5_LANE_CHUNK = 512


def _s5_kernel(u_ref, bre_ref, bim_ref, cre_ref, cim_ref, coef_ref, pre_ref, pim_ref,
               d_ref, wglu_ref, h0re_ref, h0im_ref,
               o_ref, hre_ref, him_ref, xr_scr, xi_scr, c_scr,
               *, tb, n_tb, per_group, t_valid):
    tstep = pl.program_id(1)
    u = u_ref[...]
    ub = u.astype(BF16)
    xr_raw = _dot(ub, bre_ref[...])
    xi_raw = _dot(ub, bim_ref[...])
    cfr = coef_ref[0:1]
    cfi = coef_ref[1:2]
    xr_scr[...] = cfr * xr_raw - cfi * xi_raw
    xi_scr[...] = cfr * xi_raw + cfi * xr_raw

    if not per_group:
        @pl.when(tstep == 0)
        def _():
            c_scr[0:1] = h0re_ref[0]
            c_scr[1:2] = h0im_ref[0]

    row = lax.broadcasted_iota(jnp.int32, (SUBLANES, S5_LANE_CHUNK), 0)
    for lc in range(S5_N // S5_LANE_CHUNK):
        ls = slice(lc * S5_LANE_CHUNK, (lc + 1) * S5_LANE_CHUNK)
        pr = pre_ref[:, ls]
        pi = pim_ref[:, ls]

        def body(g, carry, ls=ls, pr=pr, pi=pi):
            if per_group:
                cr = h0re_ref[g][:, ls]
                ci = h0im_ref[g][:, ls]
            else:
                cr, ci = carry
            rs = pl.ds(pl.multiple_of(g * SUBLANES, SUBLANES), SUBLANES)
            xr = xr_scr[rs, ls]
            xi = xi_scr[rs, ls]
            for s in (1, 2, 4):
                ar = pr[s - 1:s]
                ai = pi[s - 1:s]
                sr = jnp.where(row >= s, pltpu.roll(xr, s, axis=0), 0.0)
                si = jnp.where(row >= s, pltpu.roll(xi, s, axis=0), 0.0)
                xr, xi = xr + (ar * sr - ai * si), xi + (ar * si + ai * sr)
            hr = xr + (pr * cr - pi * ci)
            hi = xi + (pr * ci + pi * cr)
            xr_scr[rs, ls] = hr
            xi_scr[rs, ls] = hi
            if per_group:
                hre_ref[g, :, ls] = hr[t_valid - 1:t_valid]
                him_ref[g, :, ls] = hi[t_valid - 1:t_valid]
            return hr[SUBLANES - 1:SUBLANES], hi[SUBLANES - 1:SUBLANES]

        carry0 = (c_scr[0:1, ls], c_scr[1:2, ls])
        cr, ci = lax.fori_loop(0, tb // SUBLANES, body, carry0)
        if not per_group:
            c_scr[0:1, ls] = cr
            c_scr[1:2, ls] = ci

    y = _dot(xr_scr[...].astype(BF16), cre_ref[...]) - _dot(xi_scr[...].astype(BF16), cim_ref[...])
    y = y + d_ref[...] * u
    y = y * (0.5 * (1.0 + jnp.tanh(math.sqrt(2.0 / math.pi) * (y + 0.044715 * (y * y * y)))))
    gl = _dot(y.astype(BF16), wglu_ref[...])
    o_ref[...] = gl[:, :S5_WIDTH] * _sigmoid(gl[:, S5_WIDTH:])

    if not per_group:
        @pl.when(tstep == n_tb - 1)
        def _():
            hre_ref[0] = c_scr[0:1]
            him_ref[0] = c_scr[1:2]


def _s5(z, wts, h0re, h0im, n_seq, t_len, tb, per_group, t_valid):
    bre, bim, cre, cim, coef, pre, pim, dvec, wglu = wts
    if per_group:
        grid = (1, 1)
        n_tb = 1
        tb = z.shape[0]
        hspec = pl.BlockSpec(h0re.shape, lambda b, t: (0, 0, 0))
    else:
        n_tb = t_len // tb
        grid = (n_seq, n_tb)
        hspec = pl.BlockSpec((1, 1, S5_N), lambda b, t: (b, 0, 0))
    ucol = (4 * HG_WIDTH) // S5_WIDTH
    full = lambda a: pl.BlockSpec(a.shape, lambda b, t: (0,) * a.ndim)
    return pl.pallas_call(
        functools.partial(_s5_kernel, tb=tb, n_tb=n_tb, per_group=per_group, t_valid=t_valid),
        grid=grid,
        in_specs=[pl.BlockSpec((tb, S5_WIDTH), lambda b, t: (b * n_tb + t, ucol)),
                  full(bre), full(bim), full(cre), full(cim), full(coef), full(pre), full(pim),
                  full(dvec), full(wglu), hspec, hspec],
        out_specs=[pl.BlockSpec((tb, S5_WIDTH), lambda b, t: (b * n_tb + t, 0)), hspec, hspec],
        out_shape=[jax.ShapeDtypeStruct((z.shape[0], S5_WIDTH), F32),
                   jax.ShapeDtypeStruct(h0re.shape, F32),
                   jax.ShapeDtypeStruct(h0im.shape, F32)],
        scratch_shapes=[pltpu.VMEM((tb, S5_N), F32), pltpu.VMEM((tb, S5_N), F32),
                        pltpu.VMEM((2, S5_N), F32)],
        compiler_params=_cparams("arbitrary", "arbitrary"),
    )(z, bre, bim, cre, cim, coef, pre, pim, dvec, wglu, h0re, h0im)


def _s5_weights(lam_re, lam_im, log_step, b_re, b_im, c_re, c_im, d, w_glu):
    lr, li = lam_re.astype(F32), lam_im.astype(F32)
    dt = jnp.exp(log_step.astype(F32))[:, None]
    mag = jnp.exp(lr * dt)
    ab_re, ab_im = mag * jnp.cos(li * dt), mag * jnp.sin(li * dt)
    den = lr * lr + li * li
    nr, ni = ab_re - 1.0, ab_im
    coef_re = (nr * lr + ni * li) / den
    coef_im = (ni * lr - nr * li) / den
    ar, ai = ab_re.reshape(1, S5_N), ab_im.reshape(1, S5_N)
    pr, pi = [ar], [ai]
    for _ in range(SUBLANES - 1):
        pr, pi = pr + [pr[-1] * ar - pi[-1] * ai], pi + [pr[-1] * ai + pi[-1] * ar]
    pre = jnp.concatenate(pr, axis=0)
    pim = jnp.concatenate(pi, axis=0)
    coef = jnp.concatenate([coef_re.reshape(1, S5_N), coef_im.reshape(1, S5_N)], axis=0)
    eye = jnp.eye(S5_GROUPS, dtype=F32)
    bd = lambda w: jnp.einsum('gpc,gh->gchp', w, eye).reshape(S5_WIDTH, S5_N).astype(BF16)
    cd = lambda w: jnp.einsum('gcp,gh->gphc', w, eye).reshape(S5_N, S5_WIDTH).astype(BF16)
    return (bd(b_re), bd(b_im), cd(c_re), cd(c_im), coef, pre, pim,
            d.reshape(1, S5_WIDTH).astype(F32), w_glu.astype(BF16))


def _pattn_kernel(q_ref, k_ref, v_ref, slope_ref, lam_ref, g_ref, o_ref,
                  m0, l0, a0, m1, l1, a1, *, tq, lam_init):
    qi = pl.program_id(2)
    ki = pl.program_id(3)

    @pl.when(ki == 0)
    def _():
        for m, l, a in ((m0, l0, a0), (m1, l1, a1)):
            m[...] = jnp.full(m.shape, -jnp.inf, F32)
            l[...] = jnp.zeros(l.shape, F32)
            a[...] = jnp.zeros(a.shape, F32)

    @pl.when(ki <= qi)
    def _():
        q = q_ref[...] * (DA_DK ** -0.5)
        lane = lax.broadcasted_iota(jnp.int32, q.shape, 1)
        kb = k_ref[...].astype(BF16)
        vb = v_ref[...].astype(BF16)
        row = lax.broadcasted_iota(jnp.int32, (tq, tq), 0)
        col = lax.broadcasted_iota(jnp.int32, (tq, tq), 1)
        dist = (row - col) + (qi - ki) * tq
        bias = slope_ref[0] * dist.astype(F32)
        for half, (m, l, a) in enumerate(((m0, l0, a0), (m1, l1, a1))):
            in_half = (lane >= DA_DK) if half else (lane < DA_DK)
            qh = jnp.where(in_half, q, 0.0).astype(BF16)
            s = _dot_nt(qh, kb)
            s = jnp.where(dist >= 0, s - bias, NEG_BIG)
            m_new = jnp.maximum(m[...], jnp.max(s, axis=1, keepdims=True))
            alpha = jnp.exp(m[...] - m_new)
            p = jnp.exp(s - m_new)
            l[...] = alpha * l[...] + jnp.sum(p, axis=1, keepdims=True)
            a[...] = alpha * a[...] + _dot(p.astype(BF16), vb)
            m[...] = m_new

    @pl.when(ki == qi)
    def _():
        o = a0[...] / l0[...] - lam_ref[...] * (a1[...] / l1[...])
        o = o * lax.rsqrt(jnp.mean(o * o, axis=-1, keepdims=True) + LN_EPS) * g_ref[...]
        o_ref[...] = o * (1.0 - lam_init)


def _pattn(z, slopes, lam, subln_g, n_seq, t_len, tq, lam_init):
    nq = t_len // tq
    scr = lambda w: pltpu.VMEM((tq, w), F32)
    return pl.pallas_call(
        functools.partial(_pattn_kernel, tq=tq, lam_init=lam_init),
        grid=(n_seq, DA_HEADS, nq, nq),
        in_specs=[
            pl.BlockSpec((tq, DA_DV), lambda b, h, i, j: (b * nq + i, h)),
            pl.BlockSpec((tq, DA_DV), lambda b, h, i, j: (b * nq + jnp.minimum(i, j), DA_HEADS + h)),
            pl.BlockSpec((tq, DA_DV), lambda b, h, i, j: (b * nq + jnp.minimum(i, j), 2 * DA_HEADS + h)),
            pl.BlockSpec((1, 1, 1), lambda b, h, i, j: (h, 0, 0)),
            pl.BlockSpec((1, 1), lambda b, h, i, j: (0, 0)),
            pl.BlockSpec((1, DA_DV), lambda b, h, i, j: (0, 0)),
        ],
        out_specs=pl.BlockSpec((tq, DA_DV), lambda b, h, i, j: (b * nq + i, h)),
        out_shape=jax.ShapeDtypeStruct((n_seq * t_len, DA_HEADS * DA_DV), F32),
        scratch_shapes=[scr(1), scr(1), scr(DA_DV), scr(1), scr(1), scr(DA_DV)],
        compiler_params=_cparams("arbitrary", "arbitrary", "arbitrary", "arbitrary"),
    )(z, z, z, slopes.reshape(DA_HEADS, 1, 1), lam.reshape(1, 1), subln_g.reshape(1, DA_DV))


def _sattn_kernel(pt_ref, q_ref, kn_ref, vn_ref, slope_ref, lam_ref, g_ref, kp_ref, vp_ref,
                  o_ref, m_scr, l_scr, a_scr, *, n_pages, t_new, lam_init):
    p = pl.program_id(1)
    nrow = 2 * t_new * DA_HEADS
    past = n_pages * PAGE_SIZE

    @pl.when(p == 0)
    def _():
        m_scr[...] = jnp.full(m_scr.shape, -jnp.inf, F32)
        l_scr[...] = jnp.zeros(l_scr.shape, F32)
        a_scr[...] = jnp.zeros(a_scr.shape, F32)

    q = q_ref[0] * (DA_DK ** -0.5)
    lane = lax.broadcasted_iota(jnp.int32, q.shape, 1)
    qrows = jnp.concatenate([jnp.where(lane < DA_DK, q, 0.0), jnp.where(lane >= DA_DK, q, 0.0)],
                            axis=0).astype(BF16)
    slope = slope_ref[...]

    def update(kb, vb, n_tok, kpos0, causal):
        ncol = n_tok * DA_HEADS
        r = lax.broadcasted_iota(jnp.int32, (nrow, ncol), 0)
        c = lax.broadcasted_iota(jnp.int32, (nrow, ncol), 1)
        hshift = DA_HEADS.bit_length() - 1
        t_row = jnp.bitwise_and(jnp.right_shift(r, hshift), t_new - 1)
        dist = (past + t_row) - (kpos0 + jnp.right_shift(c, hshift))
        ok = jnp.bitwise_and(r, DA_HEADS - 1) == jnp.bitwise_and(c, DA_HEADS - 1)
        if causal:
            ok = ok & (dist >= 0)
        s = _dot_nt(qrows, kb)
        s = jnp.where(ok, s - slope * dist.astype(F32), NEG_BIG)
        m_new = jnp.maximum(m_scr[...], jnp.max(s, axis=1, keepdims=True))
        alpha = jnp.exp(m_scr[...] - m_new)
        pr = jnp.exp(s - m_new)
        l_scr[...] = alpha * l_scr[...] + jnp.sum(pr, axis=1, keepdims=True)
        a_scr[...] = alpha * a_scr[...] + _dot(pr.astype(BF16), vb)
        m_scr[...] = m_new

    update(kp_ref[0, 0].astype(BF16), vp_ref[0, 0].astype(BF16), PAGE_SIZE, p * PAGE_SIZE, False)

    @pl.when(p == n_pages - 1)
    def _():
        update(kn_ref[0].astype(BF16), vn_ref[0].astype(BF16), t_new, past, True)
        on = a_scr[...] / l_scr[...]
        hr = nrow // 2
        o = on[:hr] - lam_ref[...] * on[hr:]
        o = o * lax.rsqrt(jnp.mean(o * o, axis=-1, keepdims=True) + LN_EPS) * g_ref[...]
        o_ref[0] = o * (1.0 - lam_init)


def _sattn(zq, zk, zv, cache_k, cache_v, page_table, layer, slopes, lam, subln_g, lam_init):
    n_seq, n_pages = page_table.shape
    hr = zq.shape[1]
    t_new = hr // DA_HEADS
    nrow = 2 * hr
    slope_col = jnp.tile(slopes, 2 * t_new).reshape(nrow, 1)
    seq_spec = pl.BlockSpec((1, hr, DA_DV), lambda s, p, pt: (s, 0, 0))
    page_spec = pl.BlockSpec((1, 1, PAGE_SIZE * DA_HEADS, DA_DV),
                             lambda s, p, pt: (layer, pt[s * n_pages + p], 0, 0))
    grid_spec = pltpu.PrefetchScalarGridSpec(
        num_scalar_prefetch=1,
        grid=(n_seq, n_pages),
        in_specs=[seq_spec, seq_spec, seq_spec,
                  pl.BlockSpec((nrow, 1), lambda s, p, pt: (0, 0)),
                  pl.BlockSpec((1, 1), lambda s, p, pt: (0, 0)),
                  pl.BlockSpec((1, DA_DV), lambda s, p, pt: (0, 0)),
                  page_spec, page_spec],
        out_specs=seq_spec,
        scratch_shapes=[pltpu.VMEM((nrow, 1), F32), pltpu.VMEM((nrow, 1), F32),
                        pltpu.VMEM((nrow, DA_DV), F32)],
    )
    return pl.pallas_call(
        functools.partial(_sattn_kernel, n_pages=n_pages, t_new=t_new, lam_init=lam_init),
        grid_spec=grid_spec,
        out_shape=jax.ShapeDtypeStruct(zq.shape, F32),
        compiler_params=_cparams("arbitrary", "arbitrary"),
    )(page_table.reshape(-1), zq, zk, zv, slope_col, lam.reshape(1, 1),
      subln_g.reshape(1, DA_DV), cache_k, cache_v)


def _split_bf16(a):
    hi = a.astype(BF16)
    lo = (a - hi.astype(F32)).astype(BF16)
    return hi, lo


def _router_kernel(x_ref, w_ref, b_ref, g_ref):
    xh, xl = _split_bf16(x_ref[...])
    wh, wl = _split_bf16(w_ref[...])
    logits = _dot(xh, wh) + (_dot(xl, wh) + _dot(xh, wl)) + b_ref[...]
    lane = lax.broadcasted_iota(jnp.int32, logits.shape, 1).astype(F32)
    big = float(LANES)
    is_grp = lane < MOE_GROUPS
    gl = jnp.where(is_grp, logits, -jnp.inf)
    gmax = jnp.max(gl, axis=1, keepdims=True)
    gidx = jnp.min(jnp.where(gl == gmax, lane, big), axis=1, keepdims=True)
    den = jnp.sum(jnp.where(is_grp, jnp.exp(logits - gmax), 0.0), axis=1, keepdims=True)
    p_g = 1.0 / den
    lo = ROUTE_OFF + MOE_PER_GROUP * gidx
    in_grp = (lane >= lo) & (lane < lo + MOE_PER_GROUP)
    el = jnp.where(in_grp, logits, -jnp.inf)
    v1 = jnp.max(el, axis=1, keepdims=True)
    i1 = jnp.min(jnp.where(el == v1, lane, big), axis=1, keepdims=True)
    el2 = jnp.where(lane == i1, -jnp.inf, el)
    v2 = jnp.max(el2, axis=1, keepdims=True)
    i2 = jnp.min(jnp.where(el2 == v2, lane, big), axis=1, keepdims=True)
    e2 = jnp.exp(v2 - v1)
    w1 = (1.0 / (1.0 + e2)) * p_g
    w2 = (e2 / (1.0 + e2)) * p_g
    g_ref[...] = jnp.where(lane == i1, w1, 0.0) + jnp.where(lane == i2, w2, 0.0)


def _router(x, w_route, b_route, tm):
    m, d = x.shape
    tm = min(tm, m)
    return pl.pallas_call(
        _router_kernel,
        grid=(m // tm,),
        in_specs=[pl.BlockSpec((tm, d), lambda i: (i, 0)),
                  pl.BlockSpec((d, LANES), lambda i: (0, 0)),
                  pl.BlockSpec((1, LANES), lambda i: (0, 0))],
        out_specs=pl.BlockSpec((tm, LANES), lambda i: (i, 0)),
        out_shape=jax.ShapeDtypeStruct((m, LANES), F32),
        compiler_params=_cparams("arbitrary"),
    )(x, w_route, b_route)


def _moe_kernel(x_ref, g_ref, wg_ref, wu_ref, wd_ref, lg_ref, lb_ref, o_ref, xb_scr, y_scr):
    e = pl.program_id(1)

    @pl.when(e == 0)
    def _():
        xb_scr[...] = x_ref[...].astype(BF16)
        y_scr[...] = jnp.zeros(y_scr.shape, F32)

    lane = lax.broadcasted_iota(jnp.int32, g_ref.shape, 1)
    ge = jnp.sum(jnp.where(lane == e + ROUTE_OFF, g_ref[...], 0.0), axis=1, keepdims=True)
    xb = xb_scr[...]
    h = _silu(_dot(xb, wg_ref[0])) * _dot(xb, wu_ref[0]) * ge
    y_scr[...] += _dot(h.astype(BF16), wd_ref[0])

    @pl.when(e == MOE_EXPERTS - 1)
    def _():
        o_ref[...] = _layer_norm(DN_ALPHA * x_ref[...] + y_scr[...], lg_ref[...], lb_ref[...])


def _moe_ln(x, gates, wg, wu, wd, g, b, tm):
    m, d = x.shape
    tm = min(tm, m)
    return pl.pallas_call(
        _moe_kernel,
        grid=(m // tm, MOE_EXPERTS),
        in_specs=[pl.BlockSpec((tm, d), lambda i, e: (i, 0)),
                  pl.BlockSpec((tm, LANES), lambda i, e: (i, 0)),
                  pl.BlockSpec((1, d, MOE_FF), lambda i, e: (e, 0, 0)),
                  pl.BlockSpec((1, d, MOE_FF), lambda i, e: (e, 0, 0)),
                  pl.BlockSpec((1, MOE_FF, d), lambda i, e: (e, 0, 0)),
                  pl.BlockSpec((1, d), lambda i, e: (0, 0)),
                  pl.BlockSpec((1, d), lambda i, e: (0, 0))],
        out_specs=pl.BlockSpec((tm, d), lambda i, e: (i, 0)),
        out_shape=jax.ShapeDtypeStruct((m, d), F32),
        scratch_shapes=[pltpu.VMEM((tm, d), BF16), pltpu.VMEM((tm, d), F32)],
        compiler_params=_cparams("arbitrary", "arbitrary"),
    )(x, gates, wg, wu, wd, g.reshape(1, d), b.reshape(1, d))


def _pad_time(a, n_seq, t_len, t_pad):
    w = a.shape[1]
    a = a.reshape(n_seq, t_len, w)
    a = jnp.pad(a, ((0, 0), (0, t_pad - t_len), (0, 0)))
    return a.reshape(n_seq * t_pad, w)


def _forward(x_prompt, x_sample, cache_k, cache_v, state_hgrn, state_s5_re, state_s5_im, page_table,
             w_in_ab, w_out_ab, hg_lb_param, hg_norm_g, s5_lam_re, s5_lam_im, s5_log_step,
             s5_b_re, s5_b_im, s5_c_re, s5_c_im, s5_d, s5_w_glu,
             w_in_c, w_out_c, da_lam_q1, da_lam_k1, da_lam_q2, da_lam_k2, da_subln_g,
             moe_w_group, moe_b_group, moe_w_expert, moe_b_expert, moe_w_gate, moe_w_up, moe_w_down,
             ln_g, ln_b):
    bp, tp, d = x_prompt.shape
    bs, ts, _ = x_sample.shape
    ts_pad = SUBLANES
    n_pool = cache_k.shape[1]
    xp = x_prompt.reshape(bp * tp, d)
    xs = x_sample.reshape(bs * ts, d)
    tm_p = min(512, bp * tp)
    tq = min(512, tp)
    s5_tb = min(256, tp)

    slopes = 2.0 ** (-8.0 * jnp.arange(1, DA_HEADS + 1, dtype=F32) / DA_HEADS)
    sm = jax.nn.softmax(hg_lb_param.astype(F32), axis=0)
    lbs = jnp.cumsum(sm, axis=0) - sm[:1]
    ck = cache_k.reshape(cache_k.shape[0], n_pool, PAGE_SIZE * DA_HEADS, 2 * DA_DK)
    cv = cache_v.reshape(cache_v.shape[0], n_pool, PAGE_SIZE * DA_HEADS, DA_DV)
    zeros_hg = jnp.zeros((bp, HG_HEADS, HG_DK, HG_DK), F32)
    zeros_s5 = jnp.zeros((bp, 1, S5_N), F32)

    new_k_p, new_v_p, new_hg_p, new_re_p, new_im_p = [], [], [], [], []
    new_k_s, new_v_s, new_hg_s, new_re_s, new_im_s = [], [], [], [], []
    for i in range(DEPTH):
        j = i // 2
        if i % 2 == 0:
            w_in = w_in_ab[j].astype(BF16)
            w_out = w_out_ab[j].astype(BF16)
            s5w = _s5_weights(s5_lam_re[j], s5_lam_im[j], s5_log_step[j], s5_b_re[j], s5_b_im[j],
                              s5_c_re[j], s5_c_im[j], s5_d[j], s5_w_glu[j])
            zp = _matmul(xp, w_in, tm_p)
            oa_p, hg_p = _hgrn(zp, lbs[j], hg_norm_g[j], zeros_hg, bp, tp, math.gcd(tp, HG_CHUNK), tp)
            ob_p, re_p, im_p = _s5(zp, s5w, zeros_s5, zeros_s5, bp, tp, s5_tb, False, SUBLANES)
            xp = _proj_ln([oa_p, ob_p], [w_out[:HG_WIDTH], w_out[HG_WIDTH:]], xp,
                          ln_g[i, 0], ln_b[i, 0], tm_p)
            zs = _pad_time(_matmul(xs, w_in, bs * ts), bs, ts, ts_pad)
            s0t = jnp.swapaxes(state_hgrn[j], -1, -2)
            oa_s, hg_s = _hgrn(zs, lbs[j], hg_norm_g[j], s0t, bs, ts_pad, ts_pad, ts)
            ob_s, re_s, im_s = _s5(zs, s5w, state_s5_re[j].reshape(bs, 1, S5_N),
                                   state_s5_im[j].reshape(bs, 1, S5_N), bs, ts_pad, ts_pad, True, ts)
            unpad = lambda a: a.reshape(bs, ts_pad, -1)[:, :ts].reshape(bs * ts, -1)
            xs = _proj_ln([unpad(oa_s), unpad(ob_s)], [w_out[:HG_WIDTH], w_out[HG_WIDTH:]], xs,
                          ln_g[i, 0], ln_b[i, 0], bs * ts)
            new_hg_p.append(jnp.swapaxes(hg_p, -1, -2))
            new_hg_s.append(jnp.swapaxes(hg_s, -1, -2))
            new_re_p.append(re_p.reshape(bp, S5_GROUPS, S5_STATE))
            new_im_p.append(im_p.reshape(bp, S5_GROUPS, S5_STATE))
            new_re_s.append(re_s.reshape(bs, S5_GROUPS, S5_STATE))
            new_im_s.append(im_s.reshape(bs, S5_GROUPS, S5_STATE))
        else:
            w_in = w_in_c[j].astype(BF16)
            w_out = w_out_c[j].astype(BF16)
            lam_init = 0.8 - 0.6 * math.exp(-0.3 * i)
            lam = (jnp.exp(jnp.sum(da_lam_q1[j].astype(F32) * da_lam_k1[j].astype(F32)))
                   - jnp.exp(jnp.sum(da_lam_q2[j].astype(F32) * da_lam_k2[j].astype(F32))) + lam_init)
            zp = _matmul(xp, w_in, tm_p)
            oc_p = _pattn(zp, slopes, lam, da_subln_g[j], bp, tp, tq, lam_init)
            new_k_p.append(zp[:, DA_QK:2 * DA_QK].reshape(bp, tp, DA_HEADS, 2 * DA_DK))
            new_v_p.append(zp[:, 2 * DA_QK:].reshape(bp, tp, DA_HEADS, DA_DV))
            xp = _proj_ln([oc_p], [w_out], xp, ln_g[i, 0], ln_b[i, 0], tm_p)

            zs = _matmul(xs, w_in, bs * ts)
            rows = lambda a: a.reshape(bs, ts * DA_HEADS, DA_DV)
            zq_s, zk_s, zv_s = zs[:, :DA_QK], zs[:, DA_QK:2 * DA_QK], zs[:, 2 * DA_QK:]
            oc_s = _sattn(rows(zq_s), rows(zk_s), rows(zv_s), ck, cv, page_table, j,
                          slopes, lam, da_subln_g[j], lam_init)
            new_k_s.append(zk_s.reshape(bs, ts, DA_HEADS, 2 * DA_DK))
            new_v_s.append(zv_s.reshape(bs, ts, DA_HEADS, DA_DV))
            xs = _proj_ln([oc_s.reshape(bs * ts, DA_HEADS * DA_DV)], [w_out], xs,
                          ln_g[i, 0], ln_b[i, 0], bs * ts)

        w_route = jnp.pad(jnp.concatenate([moe_w_group[i], moe_w_expert[i]], axis=1).astype(F32),
                          ((0, 0), (0, LANES - MOE_GROUPS - MOE_EXPERTS)))
        b_route = jnp.pad(jnp.concatenate([moe_b_group[i], moe_b_expert[i]]).astype(F32),
                          (0, LANES - MOE_GROUPS - MOE_EXPERTS)).reshape(1, LANES)
        wg = moe_w_gate[i].astype(BF16)
        wu = moe_w_up[i].astype(BF16)
        wd = moe_w_down[i].astype(BF16)
        xp = _moe_ln(xp, _router(xp, w_route, b_route, tm_p), wg, wu, wd, ln_g[i, 1], ln_b[i, 1], tm_p)
        xs = _moe_ln(xs, _router(xs, w_route, b_route, bs * ts), wg, wu, wd,
                     ln_g[i, 1], ln_b[i, 1], bs * ts)

    return (xp.reshape(bp, tp, d), xs.reshape(bs, ts, d),
            jnp.stack(new_k_p), jnp.stack(new_v_p), jnp.stack(new_hg_p),
            jnp.stack(new_re_p), jnp.stack(new_im_p),
            jnp.stack(new_k_s), jnp.stack(new_v_s), jnp.stack(new_hg_s),
            jnp.stack(new_re_s), jnp.stack(new_im_s))


_forward_jit = jax.jit(_forward)


def kernel(x_prompt, x_sample, cache_k, cache_v, state_hgrn, state_s5_re, state_s5_im, page_table, w_in_ab, w_out_ab, hg_lb_param, hg_norm_g, s5_lam_re, s5_lam_im, s5_log_step, s5_b_re, s5_b_im, s5_c_re, s5_c_im, s5_d, s5_w_glu, w_in_c, w_out_c, da_lam_q1, da_lam_k1, da_lam_q2, da_lam_k2, da_subln_g, moe_w_group, moe_b_group, moe_w_expert, moe_b_expert, moe_w_gate, moe_w_up, moe_w_down, ln_g, ln_b):
    return _forward_jit(x_prompt, x_sample, cache_k, cache_v, state_hgrn, state_s5_re, state_s5_im, page_table, w_in_ab, w_out_ab, hg_lb_param, hg_norm_g, s5_lam_re, s5_lam_im, s5_log_step, s5_b_re, s5_b_im, s5_c_re, s5_c_im, s5_d, s5_w_glu, w_in_c, w_out_c, da_lam_q1, da_lam_k1, da_lam_q2, da_lam_k2, da_subln_g, moe_w_group, moe_b_group, moe_w_expert, moe_b_expert, moe_w_gate, moe_w_up, moe_w_down, ln_g, ln_b)
```

```python
import functools
import math

import jax
import jax.numpy as jnp
from jax import lax
from jax.experimental import pallas as pl
from jax.experimental.pallas import tpu as pltpu

F32 = jnp.float32
BF16 = jnp.bfloat16

D_MODEL = 1024
DEPTH = 4
HG_HEADS = 4
HG_DK = 128
HG_WIDTH = HG_HEADS * HG_DK
S5_WIDTH = 512
S5_GROUPS = 32
S5_GROUP = 16
S5_STATE = 64
S5_N = S5_GROUPS * S5_STATE
DA_HEADS = 8
DA_DK = 64
DA_DV = 128
DA_QK = DA_HEADS * 2 * DA_DK
MOE_GROUPS = 4
MOE_PER_GROUP = 4
MOE_EXPERTS = 16
MOE_FF = 512
PAGE_SIZE = 128
DN_ALPHA = (2.0 * DEPTH) ** 0.25
LN_EPS = 1e-5
NEG_BIG = -1e30
LB_FLOOR = 1e-30

SUBLANES = 8
LANES = 128
VMEM_LIMIT = 48 * 1024 * 1024

HG_CHUNK = 64
ROUTE_OFF = MOE_GROUPS


def _cparams(*sem):
    return pltpu.CompilerParams(dimension_semantics=sem, vmem_limit_bytes=VMEM_LIMIT)


def _dot(a, b):
    return jnp.dot(a, b, preferred_element_type=F32)


def _dot_nt(a, b):
    return lax.dot_general(a, b, (((1,), (1,)), ((), ())), preferred_element_type=F32)


def _dot_tn(a, b):
    return lax.dot_general(a, b, (((0,), (0,)), ((), ())), preferred_element_type=F32)


def _sigmoid(x):
    return 1.0 / (1.0 + jnp.exp(-x))


def _silu(x):
    return x * _sigmoid(x)


def _expm1(x):
    u = jnp.exp(x)
    um1 = u - 1.0
    r = jnp.where(u == 1.0, x, um1 * x / jnp.log(u))
    return jnp.where(um1 == -1.0, -1.0, r)


def _layer_norm(y, g, b):
    mu = jnp.mean(y, axis=-1, keepdims=True)
    yc = y - mu
    var = jnp.mean(yc * yc, axis=-1, keepdims=True)
    return yc * lax.rsqrt(var + LN_EPS) * g + b


def _mm_kernel(x_ref, w_ref, o_ref):
    o_ref[...] = _dot(x_ref[...].astype(BF16), w_ref[...])


def _matmul(x, w, tm):
    m, k = x.shape
    n = w.shape[1]
    tm = min(tm, m)
    return pl.pallas_call(
        _mm_kernel,
        grid=(m // tm,),
        in_specs=[pl.BlockSpec((tm, k), lambda i: (i, 0)),
                  pl.BlockSpec((k, n), lambda i: (0, 0))],
        out_specs=pl.BlockSpec((tm, n), lambda i: (i, 0)),
        out_shape=jax.ShapeDtypeStruct((m, n), F32),
        compiler_params=_cparams("arbitrary"),
    )(x, w)


def _proj_ln_kernel(*refs, n_lhs):
    a_refs = refs[:n_lhs]
    w_refs = refs[n_lhs:2 * n_lhs]
    x_ref, g_ref, b_ref, o_ref = refs[2 * n_lhs:]
    acc = _dot(a_refs[0][...].astype(BF16), w_refs[0][...])
    for a_ref, w_ref in zip(a_refs[1:], w_refs[1:]):
        acc = acc + _dot(a_ref[...].astype(BF16), w_ref[...])
    o_ref[...] = _layer_norm(DN_ALPHA * x_ref[...] + acc, g_ref[...], b_ref[...])


def _proj_ln(lhs, ws, x, g, b, tm):
    m, d = x.shape
    tm = min(tm, m)
    n_lhs = len(lhs)
    in_specs = [pl.BlockSpec((tm, a.shape[1]), lambda i: (i, 0)) for a in lhs]
    in_specs += [pl.BlockSpec(w.shape, lambda i: (0, 0)) for w in ws]
    in_specs += [pl.BlockSpec((tm, d), lambda i: (i, 0)),
                 pl.BlockSpec((1, d), lambda i: (0, 0)),
                 pl.BlockSpec((1, d), lambda i: (0, 0))]
    return pl.pallas_call(
        functools.partial(_proj_ln_kernel, n_lhs=n_lhs),
        grid=(m // tm,),
        in_specs=in_specs,
        out_specs=pl.BlockSpec((tm, d), lambda i: (i, 0)),
        out_shape=jax.ShapeDtypeStruct((m, d), F32),
        compiler_params=_cparams("arbitrary"),
    )(*lhs, *ws, x, g.reshape(1, d), b.reshape(1, d))


def _cumsum_rows(x, row):
    n = x.shape[0]
    s = 1
    while s < n:
        x = x + jnp.where(row >= s, pltpu.roll(x, s, axis=0), 0.0)
        s *= 2
    return x


def _hgrn_kernel(zq_ref, zf_ref, zi_ref, zg_ref, lb_ref, ng_ref, s0_ref,
                 o_ref, st_ref, st_scr, *, chunk, t_valid, n_chunks):
    c = pl.program_id(1)
    nb = chunk // SUBLANES

    @pl.when(c == 0)
    def _():
        st_scr[...] = s0_ref[0]

    row = lax.broadcasted_iota(jnp.int32, (chunk, HG_DK), 0)
    row8 = lax.broadcasted_iota(jnp.int32, (SUBLANES, HG_DK), 0)
    rowa = lax.broadcasted_iota(jnp.int32, (SUBLANES, chunk), 0)
    cola = lax.broadcasted_iota(jnp.int32, (SUBLANES, chunk), 1)

    for h in range(HG_HEADS):
        sl = slice(h * HG_DK, (h + 1) * HG_DK)
        zf = zf_ref[:, sl]
        zq = zq_ref[:, sl]
        zg = zg_ref[:, sl]
        v = zi_ref[:, sl]
        lb = lb_ref[:, sl]
        log_lb = jnp.log(jnp.maximum(lb, LB_FLOOR))
        log_sig = jnp.minimum(zf, 0.0) - jnp.log1p(jnp.exp(-jnp.abs(zf)))
        t = jnp.log1p(-lb) + log_sig
        logf = jnp.maximum(log_lb, t) + jnp.log1p(jnp.exp(-jnp.abs(log_lb - t)))
        k = -_expm1(logf)
        q = _silu(zq)
        if t_valid < chunk:
            valid = row < t_valid
            logf = jnp.where(valid, logf, 0.0)
            k = jnp.where(valid, k, 0.0)
        b = _cumsum_rows(logf, row)
        b_last = b[chunk - 1:chunk]
        st = st_scr[h]
        vb = v.astype(BF16)

        o = _dot_nt((q * jnp.exp(b)).astype(BF16), st.astype(BF16))
        blocks = []
        for i in range(nb):
            rs = slice(i * SUBLANES, (i + 1) * SUBLANES)
            qi, ki, bi = q[rs], k[rs], b[rs]
            if i > 0:
                r = b[i * SUBLANES - 1:i * SUBLANES]
                qt = qi * jnp.exp(bi - r)
                kt = k * jnp.exp(jnp.minimum(r - b, 0.0))
                att = _dot_nt(qt.astype(BF16), kt.astype(BF16))
                att = jnp.where(cola < i * SUBLANES, att, 0.0)
            else:
                att = jnp.zeros((SUBLANES, chunk), F32)
            for d in range(SUBLANES):
                if d == 0:
                    prod = qi * ki
                else:
                    kd = pltpu.roll(ki, d, axis=0)
                    bd = pltpu.roll(bi, d, axis=0)
                    dec = jnp.exp(jnp.where(row8 >= d, bi - bd, NEG_BIG))
                    prod = qi * kd * dec
                a = jnp.sum(prod, axis=1, keepdims=True)
                att = att + jnp.where(cola == rowa + (i * SUBLANES - d), a, 0.0)
            blocks.append(att)
        att = blocks[0] if nb == 1 else jnp.concatenate(blocks, axis=0)
        o = o + _dot(att.astype(BF16), vb)

        ke = k * jnp.exp(b_last - b)
        st_scr[h] = jnp.exp(b_last) * st + _dot_tn(vb, ke.astype(BF16))

        on = o * lax.rsqrt(jnp.mean(o * o, axis=-1, keepdims=True) + LN_EPS) * ng_ref[:, sl]
        o_ref[:, sl] = on * _silu(zg)

    @pl.when(c == n_chunks - 1)
    def _():
        st_ref[0] = st_scr[...]


def _hgrn(z, lb, norm_g, s0t, n_seq, t_len, chunk, t_valid):
    n_chunks = t_len // chunk
    zspec = lambda cb: pl.BlockSpec((chunk, HG_WIDTH), lambda b, c: (b * n_chunks + c, cb))
    pspec = pl.BlockSpec((1, HG_WIDTH), lambda b, c: (0, 0))
    sspec = pl.BlockSpec((1, HG_HEADS, HG_DK, HG_DK), lambda b, c: (b, 0, 0, 0))
    return pl.pallas_call(
        functools.partial(_hgrn_kernel, chunk=chunk, t_valid=t_valid, n_chunks=n_chunks),
        grid=(n_seq, n_chunks),
        in_specs=[zspec(0), zspec(1), zspec(2), zspec(3), pspec, pspec, sspec],
        out_specs=[pl.BlockSpec((chunk, HG_WIDTH), lambda b, c: (b * n_chunks + c, 0)), sspec],
        out_shape=[jax.ShapeDtypeStruct((n_seq * t_len, HG_WIDTH), F32),
                   jax.ShapeDtypeStruct(s0t.shape, F32)],
        scratch_shapes=[pltpu.VMEM((HG_HEADS, HG_DK, HG_DK), F32)],
        compiler_params=_cparams("arbitrary", "arbitrary"),
    )(z, z, z, z, lb.reshape(1, HG_WIDTH), norm_g.reshape(1, HG_WIDTH), s0t)


S5_LANE_CHUNK = 512


def _s5_kernel(u_ref, bre_ref, bim_ref, cre_ref, cim_ref, coef_ref, pre_ref, pim_ref,
               d_ref, wglu_ref, h0re_ref, h0im_ref,
               o_ref, hre_ref, him_ref, xr_scr, xi_scr, c_scr,
               *, tb, n_tb, per_group, t_valid):
    tstep = pl.program_id(1)
    u = u_ref[...]
    ub = u.astype(BF16)
    xr_raw = _dot(ub, bre_ref[...])
    xi_raw = _dot(ub, bim_ref[...])
    cfr = coef_ref[0:1]
    cfi = coef_ref[1:2]
    xr_scr[...] = cfr * xr_raw - cfi * xi_raw
    xi_scr[...] = cfr * xi_raw + cfi * xr_raw

    if not per_group:
        @pl.when(tstep == 0)
        def _():
            c_scr[0:1] = h0re_ref[0]
            c_scr[1:2] = h0im_ref[0]

    row = lax.broadcasted_iota(jnp.int32, (SUBLANES, S5_LANE_CHUNK), 0)
    for lc in range(S5_N // S5_LANE_CHUNK):
        ls = slice(lc * S5_LANE_CHUNK, (lc + 1) * S5_LANE_CHUNK)
        pr = pre_ref[:, ls]
        pi = pim_ref[:, ls]

        def body(g, carry, ls=ls, pr=pr, pi=pi):
            if per_group:
                cr = h0re_ref[g][:, ls]
                ci = h0im_ref[g][:, ls]
            else:
                cr, ci = carry
            rs = pl.ds(pl.multiple_of(g * SUBLANES, SUBLANES), SUBLANES)
            xr = xr_scr[rs, ls]
            xi = xi_scr[rs, ls]
            for s in (1, 2, 4):
                ar = pr[s - 1:s]
                ai = pi[s - 1:s]
                sr = jnp.where(row >= s, pltpu.roll(xr, s, axis=0), 0.0)
                si = jnp.where(row >= s, pltpu.roll(xi, s, axis=0), 0.0)
                xr, xi = xr + (ar * sr - ai * si), xi + (ar * si + ai * sr)
            hr = xr + (pr * cr - pi * ci)
            hi = xi + (pr * ci + pi * cr)
            xr_scr[rs, ls] = hr
            xi_scr[rs, ls] = hi
            if per_group:
                hre_ref[g, :, ls] = hr[t_valid - 1:t_valid]
                him_ref[g, :, ls] = hi[t_valid - 1:t_valid]
            return hr[SUBLANES - 1:SUBLANES], hi[SUBLANES - 1:SUBLANES]

        carry0 = (c_scr[0:1, ls], c_scr[1:2, ls])
        cr, ci = lax.fori_loop(0, tb // SUBLANES, body, carry0)
        if not per_group:
            c_scr[0:1, ls] = cr
            c_scr[1:2, ls] = ci

    y = _dot(xr_scr[...].astype(BF16), cre_ref[...]) - _dot(xi_scr[...].astype(BF16), cim_ref[...])
    y = y + d_ref[...] * u
    y = y * (0.5 * (1.0 + jnp.tanh(math.sqrt(2.0 / math.pi) * (y + 0.044715 * (y * y * y)))))
    gl = _dot(y.astype(BF16), wglu_ref[...])
    o_ref[...] = gl[:, :S5_WIDTH] * _sigmoid(gl[:, S5_WIDTH:])

    if not per_group:
        @pl.when(tstep == n_tb - 1)
        def _():
            hre_ref[0] = c_scr[0:1]
            him_ref[0] = c_scr[1:2]


def _s5(z, wts, h0re, h0im, n_seq, t_len, tb, per_group, t_valid):
    bre, bim, cre, cim, coef, pre, pim, dvec, wglu = wts
    if per_group:
        grid = (1, 1)
        n_tb = 1
        tb = z.shape[0]
        hspec = pl.BlockSpec(h0re.shape, lambda b, t: (0, 0, 0))
    else:
        n_tb = t_len // tb
        grid = (n_seq, n_tb)
        hspec = pl.BlockSpec((1, 1, S5_N), lambda b, t: (b, 0, 0))
    ucol = (4 * HG_WIDTH) // S5_WIDTH
    full = lambda a: pl.BlockSpec(a.shape, lambda b, t: (0,) * a.ndim)
    return pl.pallas_call(
        functools.partial(_s5_kernel, tb=tb, n_tb=n_tb, per_group=per_group, t_valid=t_valid),
        grid=grid,
        in_specs=[pl.BlockSpec((tb, S5_WIDTH), lambda b, t: (b * n_tb + t, ucol)),
                  full(bre), full(bim), full(cre), full(cim), full(coef), full(pre), full(pim),
                  full(dvec), full(wglu), hspec, hspec],
        out_specs=[pl.BlockSpec((tb, S5_WIDTH), lambda b, t: (b * n_tb + t, 0)), hspec, hspec],
        out_shape=[jax.ShapeDtypeStruct((z.shape[0], S5_WIDTH), F32),
                   jax.ShapeDtypeStruct(h0re.shape, F32),
                   jax.ShapeDtypeStruct(h0im.shape, F32)],
        scratch_shapes=[pltpu.VMEM((tb, S5_N), F32), pltpu.VMEM((tb, S5_N), F32),
                        pltpu.VMEM((2, S5_N), F32)],
        compiler_params=_cparams("arbitrary", "arbitrary"),
    )(z, bre, bim, cre, cim, coef, pre, pim, dvec, wglu, h0re, h0im)


def _s5_weights(lam_re, lam_im, log_step, b_re, b_im, c_re, c_im, d, w_glu):
    lr, li = lam_re.astype(F32), lam_im.astype(F32)
    dt = jnp.exp(log_step.astype(F32))[:, None]
    mag = jnp.exp(lr * dt)
    ab_re, ab_im = mag * jnp.cos(li * dt), mag * jnp.sin(li * dt)
    den = lr * lr + li * li
    nr, ni = ab_re - 1.0, ab_im
    coef_re = (nr * lr + ni * li) / den
    coef_im = (ni * lr - nr * li) / den
    ar, ai = ab_re.reshape(1, S5_N), ab_im.reshape(1, S5_N)
    pr, pi = [ar], [ai]
    for _ in range(SUBLANES - 1):
        pr, pi = pr + [pr[-1] * ar - pi[-1] * ai], pi + [pr[-1] * ai + pi[-1] * ar]
    pre = jnp.concatenate(pr, axis=0)
    pim = jnp.concatenate(pi, axis=0)
    coef = jnp.concatenate([coef_re.reshape(1, S5_N), coef_im.reshape(1, S5_N)], axis=0)
    eye = jnp.eye(S5_GROUPS, dtype=F32)
    bd = lambda w: jnp.einsum('gpc,gh->gchp', w, eye).reshape(S5_WIDTH, S5_N).astype(BF16)
    cd = lambda w: jnp.einsum('gcp,gh->gphc', w, eye).reshape(S5_N, S5_WIDTH).astype(BF16)
    return (bd(b_re), bd(b_im), cd(c_re), cd(c_im), coef, pre, pim,
            d.reshape(1, S5_WIDTH).astype(F32), w_glu.astype(BF16))


ALIBI_SPLIT_BITS = 7


def _qkv_kernel(*refs):
    x_ref, w_ref, wvt_ref = refs[:3]
    q_ref, k_ref, v_ref, vt_ref = refs[-4:]
    xb = x_ref[...].astype(BF16)
    z = _dot(xb, w_ref[...])
    q_ref[...] = z[:, :DA_QK]
    k_ref[0] = z[:, DA_QK:2 * DA_QK]
    v_ref[0] = z[:, 2 * DA_QK:]
    vt_ref[...] = _dot_nt(wvt_ref[...], xb).astype(BF16)


def _qkv_proj(x, w, kv_bufs, layer, n_layers, tm):
    m, d = x.shape
    tm = min(tm, m)
    wvt = jnp.transpose(w[:, 2 * DA_QK:])
    in_specs = [pl.BlockSpec((tm, d), lambda i: (i, 0)), pl.BlockSpec(w.shape, lambda i: (0, 0)),
                pl.BlockSpec(wvt.shape, lambda i: (0, 0))]
    args = [x, w, wvt]
    aliases = {}
    if kv_bufs is not None:
        in_specs += [pl.BlockSpec(memory_space=pl.ANY)] * 2
        args += list(kv_bufs)
        aliases = {3: 1, 4: 2}
    kv_spec = pl.BlockSpec((1, tm, DA_QK), lambda i: (layer, i, 0))
    kv_shape = jax.ShapeDtypeStruct((n_layers, m, DA_QK), F32)
    return pl.pallas_call(
        _qkv_kernel,
        grid=(m // tm,),
        in_specs=in_specs,
        out_specs=[pl.BlockSpec((tm, DA_QK), lambda i: (i, 0)), kv_spec, kv_spec,
                   pl.BlockSpec((DA_HEADS * DA_DV, tm), lambda i: (0, i))],
        out_shape=[jax.ShapeDtypeStruct((m, DA_QK), F32), kv_shape, kv_shape,
                   jax.ShapeDtypeStruct((DA_HEADS * DA_DV, m), BF16)],
        input_output_aliases=aliases,
        compiler_params=_cparams("arbitrary"),
    )(*args)


def _pattn_kernel(qi_ref, ki_ref, q_ref, k_ref, vt_ref, slope_ref, lam_ref, g_ref, o_ref,
                  m0, l0, a0, m1, l1, a1, *, tq, lam_init):
    step = pl.program_id(2)
    qi = qi_ref[step]
    ki = ki_ref[step]
    stats = ((m0, l0, a0), (m1, l1, a1))

    @pl.when(ki == 0)
    def _():
        for m, l, a in stats:
            m[...] = jnp.full(m.shape, -jnp.inf, F32)
            l[...] = jnp.zeros(l.shape, F32)
            a[...] = jnp.zeros(a.shape, F32)

    q = q_ref[...] * (DA_DK ** -0.5)
    k = k_ref[0]
    lane = lax.broadcasted_iota(jnp.int32, q.shape, 1)
    kpos = ki * tq + lax.broadcasted_iota(jnp.int32, k.shape, 0)
    lo_mask = (1 << ALIBI_SPLIT_BITS) - 1
    slope = slope_ref[0]
    b_hi = slope * (kpos - jnp.bitwise_and(kpos, lo_mask)).astype(F32)
    b_lo = slope * jnp.bitwise_and(kpos, lo_mask).astype(F32)
    qs, ks = [], []
    for half in range(2):
        in_half = (lane >= DA_DK) if half else (lane < DA_DK)
        spare = 0 if half else DA_DK
        qs.append(jnp.where(in_half, q, jnp.where((lane == spare) | (lane == spare + 1), 1.0, 0.0))
                  .astype(BF16))
        ks.append(jnp.where(in_half, k, jnp.where(lane == spare, b_hi,
                                                  jnp.where(lane == spare + 1, b_lo, 0.0)))
                  .astype(BF16))
    vt = vt_ref[...]

    def update(masked):
        if masked:
            krow = lax.broadcasted_iota(jnp.int32, (tq, tq), 0)
            qcol = lax.broadcasted_iota(jnp.int32, (tq, tq), 1)
            keep = krow <= qcol
        for qh, kh, (m, l, a) in zip(qs, ks, stats):
            st = _dot_nt(kh, qh)
            if masked:
                st = jnp.where(keep, st, NEG_BIG)
            m_new = jnp.maximum(m[...], jnp.max(st, axis=0, keepdims=True))
            alpha = jnp.exp(m[...] - m_new)
            pt = jnp.exp(st - m_new)
            l[...] = alpha * l[...] + jnp.sum(pt, axis=0, keepdims=True)
            a[...] = alpha * a[...] + _dot(vt, pt.astype(BF16))
            m[...] = m_new

    @pl.when(ki < qi)
    def _():
        update(False)

    @pl.when(ki == qi)
    def _():
        update(True)
        ot = a0[...] / l0[...] - lam_ref[...] * (a1[...] / l1[...])
        ot = ot * lax.rsqrt(jnp.mean(ot * ot, axis=0, keepdims=True) + LN_EPS)
        o_ref[...] = jnp.transpose(ot) * (g_ref[...] * (1.0 - lam_init))


def _pattn(q, kbuf, vt, layer, slopes, lam, subln_g, n_seq, t_len, tq, lam_init):
    nq = t_len // tq
    pairs = [(i, j) for i in range(nq) for j in range(i + 1)]
    qi_tab = jnp.array([p[0] for p in pairs], jnp.int32)
    ki_tab = jnp.array([p[1] for p in pairs], jnp.int32)
    q_spec = pl.BlockSpec((tq, DA_DV), lambda b, h, s, qt, kt: (b * nq + qt[s], h))
    grid_spec = pltpu.PrefetchScalarGridSpec(
        num_scalar_prefetch=2,
        grid=(n_seq, DA_HEADS, len(pairs)),
        in_specs=[q_spec,
                  pl.BlockSpec((1, tq, DA_DV), lambda b, h, s, qt, kt: (layer, b * nq + kt[s], h)),
                  pl.BlockSpec((DA_DV, tq), lambda b, h, s, qt, kt: (h, b * nq + kt[s])),
                  pl.BlockSpec((1, 1, 1), lambda b, h, s, qt, kt: (h, 0, 0)),
                  pl.BlockSpec((1, 1), lambda b, h, s, qt, kt: (0, 0)),
                  pl.BlockSpec((1, DA_DV), lambda b, h, s, qt, kt: (0, 0))],
        out_specs=q_spec,
        scratch_shapes=[pltpu.VMEM((1, tq), F32), pltpu.VMEM((1, tq), F32), pltpu.VMEM((DA_DV, tq), F32)] * 2,
    )
    return pl.pallas_call(
        functools.partial(_pattn_kernel, tq=tq, lam_init=lam_init),
        grid_spec=grid_spec,
        out_shape=jax.ShapeDtypeStruct((n_seq * t_len, DA_HEADS * DA_DV), F32),
        compiler_params=_cparams("arbitrary", "arbitrary", "arbitrary"),
    )(qi_tab, ki_tab, q, kbuf, vt, slopes.reshape(DA_HEADS, 1, 1), lam.reshape(1, 1),
      subln_g.reshape(1, DA_DV))


MAX_PAGES_PER_STEP = 8


def _sattn_kernel(pt_ref, q_ref, kn_ref, vn_ref, cmat_ref, slope_ref, lam_ref, g_ref, *rest,
                  n_pages, pps, t_new, lam_init):
    kp_refs = rest[:pps]
    vp_refs = rest[pps:2 * pps]
    o_ref, m_scr, l_scr, a_scr = rest[2 * pps:]
    p = pl.program_id(1)
    nrow = 2 * t_new * DA_HEADS

    @pl.when(p == 0)
    def _():
        m_scr[...] = jnp.full(m_scr.shape, -jnp.inf, F32)
        l_scr[...] = jnp.zeros(l_scr.shape, F32)
        a_scr[...] = jnp.zeros(a_scr.shape, F32)

    q = q_ref[0] * (DA_DK ** -0.5)
    lane = lax.broadcasted_iota(jnp.int32, q.shape, 1)
    qrows = jnp.concatenate([jnp.where(lane < DA_DK, q, 0.0), jnp.where(lane >= DA_DK, q, 0.0)],
                            axis=0).astype(BF16)
    slope = slope_ref[...]

    def accumulate(scores, values):
        m_old = m_scr[...]
        m_new = m_old
        for s in scores:
            m_new = jnp.maximum(m_new, jnp.max(s, axis=1, keepdims=True))
        alpha = jnp.exp(m_old - m_new)
        l_new = alpha * l_scr[...]
        acc = alpha * a_scr[...]
        for s, vb in zip(scores, values):
            pr = jnp.exp(s - m_new)
            l_new = l_new + jnp.sum(pr, axis=1, keepdims=True)
            acc = acc + _dot(pr.astype(BF16), vb)
        m_scr[...] = m_new
        l_scr[...] = l_new
        a_scr[...] = acc

    cmat = cmat_ref[...]
    scores, values = [], []
    for i in range(pps):
        page_start = ((p * pps + i) * PAGE_SIZE).astype(F32)
        s = _dot_nt(qrows, kp_refs[i][0, 0].astype(BF16))
        scores.append(s + (cmat + slope * page_start))
        values.append(vp_refs[i][0, 0].astype(BF16))
    accumulate(scores, values)

    @pl.when(p == n_pages // pps - 1)
    def _():
        ncol = t_new * DA_HEADS
        r = lax.broadcasted_iota(jnp.int32, (nrow, ncol), 0)
        c = lax.broadcasted_iota(jnp.int32, (nrow, ncol), 1)
        hshift = DA_HEADS.bit_length() - 1
        dist = jnp.bitwise_and(jnp.right_shift(r, hshift), t_new - 1) - jnp.right_shift(c, hshift)
        ok = (jnp.bitwise_and(r, DA_HEADS - 1) == jnp.bitwise_and(c, DA_HEADS - 1)) & (dist >= 0)
        s = _dot_nt(qrows, kn_ref[0].astype(BF16))
        s = jnp.where(ok, s - slope * dist.astype(F32), NEG_BIG)
        accumulate([s], [vn_ref[0].astype(BF16)])
        on = a_scr[...] / l_scr[...]
        hr = nrow // 2
        o = on[:hr] - lam_ref[...] * on[hr:]
        o = o * lax.rsqrt(jnp.mean(o * o, axis=-1, keepdims=True) + LN_EPS) * g_ref[...]
        o_ref[0] = o * (1.0 - lam_init)


def _sattn(zq, zk, zv, cache_k, cache_v, page_table, layer, slopes, lam, subln_g, lam_init):
    n_seq, n_pages = page_table.shape
    pps = math.gcd(n_pages, MAX_PAGES_PER_STEP)
    hr = zq.shape[1]
    t_new = hr // DA_HEADS
    nrow = 2 * hr
    past = n_pages * PAGE_SIZE
    slope_col = jnp.tile(slopes, 2 * t_new).reshape(nrow, 1)
    r = jnp.arange(nrow)[:, None]
    c = jnp.arange(PAGE_SIZE * DA_HEADS)[None, :]
    dist0 = (past + (r // DA_HEADS) % t_new - c // DA_HEADS).astype(F32)
    cmat = jnp.where(r % DA_HEADS == c % DA_HEADS, -slope_col * dist0, NEG_BIG).astype(F32)
    seq_spec = pl.BlockSpec((1, hr, DA_DV), lambda s, p, pt: (s, 0, 0))
    const = lambda shape: pl.BlockSpec(shape, lambda s, p, pt: (0,) * len(shape))

    def page_spec(i):
        return pl.BlockSpec((1, 1, PAGE_SIZE * DA_HEADS, DA_DV),
                            lambda s, p, pt: (layer, pt[s * n_pages + p * pps + i], 0, 0))

    grid_spec = pltpu.PrefetchScalarGridSpec(
        num_scalar_prefetch=1,
        grid=(n_seq, n_pages // pps),
        in_specs=[seq_spec, seq_spec, seq_spec, const(cmat.shape), const((nrow, 1)), const((1, 1)),
                  const((1, DA_DV))] + [page_spec(i) for i in range(pps)] * 2,
        out_specs=seq_spec,
        scratch_shapes=[pltpu.VMEM((nrow, 1), F32), pltpu.VMEM((nrow, 1), F32),
                        pltpu.VMEM((nrow, DA_DV), F32)],
    )
    return pl.pallas_call(
        functools.partial(_sattn_kernel, n_pages=n_pages, pps=pps, t_new=t_new, lam_init=lam_init),
        grid_spec=grid_spec,
        out_shape=jax.ShapeDtypeStruct(zq.shape, F32),
        compiler_params=_cparams("arbitrary", "arbitrary"),
    )(page_table.reshape(-1), zq, zk, zv, cmat, slope_col, lam.reshape(1, 1),
      subln_g.reshape(1, DA_DV), *([cache_k] * pps), *([cache_v] * pps))


def _split_bf16(a):
    hi = a.astype(BF16)
    lo = (a - hi.astype(F32)).astype(BF16)
    return hi, lo


def _router_kernel(x_ref, w_ref, b_ref, g_ref):
    xh, xl = _split_bf16(x_ref[...])
    wh, wl = _split_bf16(w_ref[...])
    logits = _dot(xh, wh) + (_dot(xl, wh) + _dot(xh, wl)) + b_ref[...]
    lane = lax.broadcasted_iota(jnp.int32, logits.shape, 1).astype(F32)
    big = float(LANES)
    is_grp = lane < MOE_GROUPS
    gl = jnp.where(is_grp, logits, -jnp.inf)
    gmax = jnp.max(gl, axis=1, keepdims=True)
    gidx = jnp.min(jnp.where(gl == gmax, lane, big), axis=1, keepdims=True)
    den = jnp.sum(jnp.where(is_grp, jnp.exp(logits - gmax), 0.0), axis=1, keepdims=True)
    p_g = 1.0 / den
    lo = ROUTE_OFF + MOE_PER_GROUP * gidx
    in_grp = (lane >= lo) & (lane < lo + MOE_PER_GROUP)
    el = jnp.where(in_grp, logits, -jnp.inf)
    v1 = jnp.max(el, axis=1, keepdims=True)
    i1 = jnp.min(jnp.where(el == v1, lane, big), axis=1, keepdims=True)
    el2 = jnp.where(lane == i1, -jnp.inf, el)
    v2 = jnp.max(el2, axis=1, keepdims=True)
    i2 = jnp.min(jnp.where(el2 == v2, lane, big), axis=1, keepdims=True)
    e2 = jnp.exp(v2 - v1)
    w1 = (1.0 / (1.0 + e2)) * p_g
    w2 = (e2 / (1.0 + e2)) * p_g
    g_ref[...] = jnp.where(lane == i1, w1, 0.0) + jnp.where(lane == i2, w2, 0.0)


def _router(x, w_route, b_route, tm):
    m, d = x.shape
    tm = min(tm, m)
    return pl.pallas_call(
        _router_kernel,
        grid=(m // tm,),
        in_specs=[pl.BlockSpec((tm, d), lambda i: (i, 0)),
                  pl.BlockSpec((d, LANES), lambda i: (0, 0)),
                  pl.BlockSpec((1, LANES), lambda i: (0, 0))],
        out_specs=pl.BlockSpec((tm, LANES), lambda i: (i, 0)),
        out_shape=jax.ShapeDtypeStruct((m, LANES), F32),
        compiler_params=_cparams("arbitrary"),
    )(x, w_route, b_route)


def _moe_kernel(x_ref, g_ref, wg_ref, wu_ref, wd_ref, lg_ref, lb_ref, o_ref, xb_scr, y_scr):
    e = pl.program_id(1)

    @pl.when(e == 0)
    def _():
        xb_scr[...] = x_ref[...].astype(BF16)
        y_scr[...] = jnp.zeros(y_scr.shape, F32)

    lane = lax.broadcasted_iota(jnp.int32, g_ref.shape, 1)
    ge = jnp.sum(jnp.where(lane == e + ROUTE_OFF, g_ref[...], 0.0), axis=1, keepdims=True)
    xb = xb_scr[...]
    h = _silu(_dot(xb, wg_ref[0])) * _dot(xb, wu_ref[0]) * ge
    y_scr[...] += _dot(h.astype(BF16), wd_ref[0])

    @pl.when(e == MOE_EXPERTS - 1)
    def _():
        o_ref[...] = _layer_norm(DN_ALPHA * x_ref[...] + y_scr[...], lg_ref[...], lb_ref[...])


def _moe_ln(x, gates, wg, wu, wd, g, b, tm):
    m, d = x.shape
    tm = min(tm, m)
    return pl.pallas_call(
        _moe_kernel,
        grid=(m // tm, MOE_EXPERTS),
        in_specs=[pl.BlockSpec((tm, d), lambda i, e: (i, 0)),
                  pl.BlockSpec((tm, LANES), lambda i, e: (i, 0)),
                  pl.BlockSpec((1, d, MOE_FF), lambda i, e: (e, 0, 0)),
                  pl.BlockSpec((1, d, MOE_FF), lambda i, e: (e, 0, 0)),
                  pl.BlockSpec((1, MOE_FF, d), lambda i, e: (e, 0, 0)),
                  pl.BlockSpec((1, d), lambda i, e: (0, 0)),
                  pl.BlockSpec((1, d), lambda i, e: (0, 0))],
        out_specs=pl.BlockSpec((tm, d), lambda i, e: (i, 0)),
        out_shape=jax.ShapeDtypeStruct((m, d), F32),
        scratch_shapes=[pltpu.VMEM((tm, d), BF16), pltpu.VMEM((tm, d), F32)],
        compiler_params=_cparams("arbitrary", "arbitrary"),
    )(x, gates, wg, wu, wd, g.reshape(1, d), b.reshape(1, d))


def _pad_time(a, n_seq, t_len, t_pad):
    w = a.shape[1]
    a = a.reshape(n_seq, t_len, w)
    a = jnp.pad(a, ((0, 0), (0, t_pad - t_len), (0, 0)))
    return a.reshape(n_seq * t_pad, w)


def _forward(x_prompt, x_sample, cache_k, cache_v, state_hgrn, state_s5_re, state_s5_im, page_table,
             w_in_ab, w_out_ab, hg_lb_param, hg_norm_g, s5_lam_re, s5_lam_im, s5_log_step,
             s5_b_re, s5_b_im, s5_c_re, s5_c_im, s5_d, s5_w_glu,
             w_in_c, w_out_c, da_lam_q1, da_lam_k1, da_lam_q2, da_lam_k2, da_subln_g,
             moe_w_group, moe_b_group, moe_w_expert, moe_b_expert, moe_w_gate, moe_w_up, moe_w_down,
             ln_g, ln_b):
    bp, tp, d = x_prompt.shape
    bs, ts, _ = x_sample.shape
    ts_pad = SUBLANES
    n_pool = cache_k.shape[1]
    xp = x_prompt.reshape(bp * tp, d)
    xs = x_sample.reshape(bs * ts, d)
    tm_p = min(512, bp * tp)
    tq = min(512, tp)
    s5_tb = min(256, tp)

    slopes = 2.0 ** (-8.0 * jnp.arange(1, DA_HEADS + 1, dtype=F32) / DA_HEADS)
    sm = jax.nn.softmax(hg_lb_param.astype(F32), axis=0)
    lbs = jnp.cumsum(sm, axis=0) - sm[:1]
    ck = cache_k.reshape(cache_k.shape[0], n_pool, PAGE_SIZE * DA_HEADS, 2 * DA_DK)
    cv = cache_v.reshape(cache_v.shape[0], n_pool, PAGE_SIZE * DA_HEADS, DA_DV)
    zeros_hg = jnp.zeros((bp, HG_HEADS, HG_DK, HG_DK), F32)
    zeros_s5 = jnp.zeros((bp, 1, S5_N), F32)

    new_hg_p, new_re_p, new_im_p = [], [], []
    kv_bufs_p = None
    new_k_s, new_v_s, new_hg_s, new_re_s, new_im_s = [], [], [], [], []
    for i in range(DEPTH):
        j = i // 2
        if i % 2 == 0:
            w_in = w_in_ab[j].astype(BF16)
            w_out = w_out_ab[j].astype(BF16)
            s5w = _s5_weights(s5_lam_re[j], s5_lam_im[j], s5_log_step[j], s5_b_re[j], s5_b_im[j],
                              s5_c_re[j], s5_c_im[j], s5_d[j], s5_w_glu[j])
            zp = _matmul(xp, w_in, tm_p)
            oa_p, hg_p = _hgrn(zp, lbs[j], hg_norm_g[j], zeros_hg, bp, tp, math.gcd(tp, HG_CHUNK), tp)
            ob_p, re_p, im_p = _s5(zp, s5w, zeros_s5, zeros_s5, bp, tp, s5_tb, False, SUBLANES)
            xp = _proj_ln([oa_p, ob_p], [w_out[:HG_WIDTH], w_out[HG_WIDTH:]], xp,
                          ln_g[i, 0], ln_b[i, 0], tm_p)
            zs = _pad_time(_matmul(xs, w_in, bs * ts), bs, ts, ts_pad)
            s0t = jnp.swapaxes(state_hgrn[j], -1, -2)
            oa_s, hg_s = _hgrn(zs, lbs[j], hg_norm_g[j], s0t, bs, ts_pad, ts_pad, ts)
            ob_s, re_s, im_s = _s5(zs, s5w, state_s5_re[j].reshape(bs, 1, S5_N),
                                   state_s5_im[j].reshape(bs, 1, S5_N), bs, ts_pad, ts_pad, True, ts)
            unpad = lambda a: a.reshape(bs, ts_pad, -1)[:, :ts].reshape(bs * ts, -1)
            xs = _proj_ln([unpad(oa_s), unpad(ob_s)], [w_out[:HG_WIDTH], w_out[HG_WIDTH:]], xs,
                          ln_g[i, 0], ln_b[i, 0], bs * ts)
            new_hg_p.append(jnp.swapaxes(hg_p, -1, -2))
            new_hg_s.append(jnp.swapaxes(hg_s, -1, -2))
            new_re_p.append(re_p.reshape(bp, S5_GROUPS, S5_STATE))
            new_im_p.append(im_p.reshape(bp, S5_GROUPS, S5_STATE))
            new_re_s.append(re_s.reshape(bs, S5_GROUPS, S5_STATE))
            new_im_s.append(im_s.reshape(bs, S5_GROUPS, S5_STATE))
        else:
            w_in = w_in_c[j].astype(BF16)
            w_out = w_out_c[j].astype(BF16)
            lam_init = 0.8 - 0.6 * math.exp(-0.3 * i)
            lam = (jnp.exp(jnp.sum(da_lam_q1[j].astype(F32) * da_lam_k1[j].astype(F32)))
                   - jnp.exp(jnp.sum(da_lam_q2[j].astype(F32) * da_lam_k2[j].astype(F32))) + lam_init)
            qp, kbuf_p, vbuf_p, vt_p = _qkv_proj(xp, w_in, kv_bufs_p, j, DEPTH // 2, tm_p)
            kv_bufs_p = (kbuf_p, vbuf_p)
            oc_p = _pattn(qp, kbuf_p, vt_p, j, slopes, lam, da_subln_g[j], bp, tp, tq, lam_init)
            xp = _proj_ln([oc_p], [w_out], xp, ln_g[i, 0], ln_b[i, 0], tm_p)

            zs = _matmul(xs, w_in, bs * ts)
            rows = lambda a: a.reshape(bs, ts * DA_HEADS, DA_DV)
            zq_s, zk_s, zv_s = zs[:, :DA_QK], zs[:, DA_QK:2 * DA_QK], zs[:, 2 * DA_QK:]
            oc_s = _sattn(rows(zq_s), rows(zk_s), rows(zv_s), ck, cv, page_table, j,
                          slopes, lam, da_subln_g[j], lam_init)
            new_k_s.append(zk_s.reshape(bs, ts, DA_HEADS, 2 * DA_DK))
            new_v_s.append(zv_s.reshape(bs, ts, DA_HEADS, DA_DV))
            xs = _proj_ln([oc_s.reshape(bs * ts, DA_HEADS * DA_DV)], [w_out], xs,
                          ln_g[i, 0], ln_b[i, 0], bs * ts)

        w_route = jnp.pad(jnp.concatenate([moe_w_group[i], moe_w_expert[i]], axis=1).astype(F32),
                          ((0, 0), (0, LANES - MOE_GROUPS - MOE_EXPERTS)))
        b_route = jnp.pad(jnp.concatenate([moe_b_group[i], moe_b_expert[i]]).astype(F32),
                          (0, LANES - MOE_GROUPS - MOE_EXPERTS)).reshape(1, LANES)
        wg = moe_w_gate[i].astype(BF16)
        wu = moe_w_up[i].astype(BF16)
        wd = moe_w_down[i].astype(BF16)
        xp = _moe_ln(xp, _router(xp, w_route, b_route, tm_p), wg, wu, wd, ln_g[i, 1], ln_b[i, 1], tm_p)
        xs = _moe_ln(xs, _router(xs, w_route, b_route, bs * ts), wg, wu, wd,
                     ln_g[i, 1], ln_b[i, 1], bs * ts)

    return (xp.reshape(bp, tp, d), xs.reshape(bs, ts, d),
            kv_bufs_p[0].reshape(DEPTH // 2, bp, tp, DA_HEADS, 2 * DA_DK),
            kv_bufs_p[1].reshape(DEPTH // 2, bp, tp, DA_HEADS, DA_DV), jnp.stack(new_hg_p),
            jnp.stack(new_re_p), jnp.stack(new_im_p),
            jnp.stack(new_k_s), jnp.stack(new_v_s), jnp.stack(new_hg_s),
            jnp.stack(new_re_s), jnp.stack(new_im_s))


_forward_jit = jax.jit(_forward)


def kernel(x_prompt, x_sample, cache_k, cache_v, state_hgrn, state_s5_re, state_s5_im, page_table, w_in_ab, w_out_ab, hg_lb_param, hg_norm_g, s5_lam_re, s5_lam_im, s5_log_step, s5_b_re, s5_b_im, s5_c_re, s5_c_im, s5_d, s5_w_glu, w_in_c, w_out_c, da_lam_q1, da_lam_k1, da_lam_q2, da_lam_k2, da_subln_g, moe_w_group, moe_b_group, moe_w_expert, moe_b_expert, moe_w_gate, moe_w_up, moe_w_down, ln_g, ln_b):
    return _forward_jit(x_prompt, x_sample, cache_k, cache_v, state_hgrn, state_s5_re, state_s5_im, page_table, w_in_ab, w_out_ab, hg_lb_param, hg_norm_g, s5_lam_re, s5_lam_im, s5_log_step, s5_b_re, s5_b_im, s5_c_re, s5_c_im, s5_d, s5_w_glu, w_in_c, w_out_c, da_lam_q1, da_lam_k1, da_lam_q2, da_lam_k2, da_subln_g, moe_w_group, moe_b_group, moe_w_expert, moe_b_expert, moe_w_gate, moe_w_up, moe_w_down, ln_g, ln_b)
```

```python
import functools
import math

import jax
import jax.numpy as jnp
from jax import lax
from jax.experimental import pallas as pl
from jax.experimental.pallas import tpu as pltpu

F32 = jnp.float32
BF16 = jnp.bfloat16

D_MODEL = 1024
DEPTH = 4
HG_HEADS = 4
HG_DK = 128
HG_WIDTH = HG_HEADS * HG_DK
S5_WIDTH = 512
S5_GROUPS = 32
S5_GROUP = 16
S5_STATE = 64
S5_N = S5_GROUPS * S5_STATE
DA_HEADS = 8
DA_DK = 64
DA_DV = 128
DA_QK = DA_HEADS * 2 * DA_DK
MOE_GROUPS = 4
MOE_PER_GROUP = 4
MOE_EXPERTS = 16
MOE_FF = 512
PAGE_SIZE = 128
DN_ALPHA = (2.0 * DEPTH) ** 0.25
LN_EPS = 1e-5
NEG_BIG = -1e30
LB_FLOOR = 1e-30

SUBLANES = 8
LANES = 128
VMEM_LIMIT = 48 * 1024 * 1024

HG_CHUNK = 64
ROUTE_OFF = MOE_GROUPS


def _cparams(*sem):
    return pltpu.CompilerParams(dimension_semantics=sem, vmem_limit_bytes=VMEM_LIMIT)


def _dot(a, b):
    return jnp.dot(a, b, preferred_element_type=F32)


def _dot_nt(a, b):
    return lax.dot_general(a, b, (((1,), (1,)), ((), ())), preferred_element_type=F32)


def _dot_tn(a, b):
    return lax.dot_general(a, b, (((0,), (0,)), ((), ())), preferred_element_type=F32)


def _sigmoid(x):
    return 1.0 / (1.0 + jnp.exp(-x))


def _silu(x):
    return x * _sigmoid(x)


def _expm1(x):
    u = jnp.exp(x)
    um1 = u - 1.0
    r = jnp.where(u == 1.0, x, um1 * x / jnp.log(u))
    return jnp.where(um1 == -1.0, -1.0, r)


def _layer_norm(y, g, b):
    mu = jnp.mean(y, axis=-1, keepdims=True)
    yc = y - mu
    var = jnp.mean(yc * yc, axis=-1, keepdims=True)
    return yc * lax.rsqrt(var + LN_EPS) * g + b


def _mm_kernel(x_ref, w_ref, o_ref):
    o_ref[...] = _dot(x_ref[...].astype(BF16), w_ref[...])


def _matmul(x, w, tm):
    m, k = x.shape
    n = w.shape[1]
    tm = min(tm, m)
    return pl.pallas_call(
        _mm_kernel,
        grid=(m // tm,),
        in_specs=[pl.BlockSpec((tm, k), lambda i: (i, 0)),
                  pl.BlockSpec((k, n), lambda i: (0, 0))],
        out_specs=pl.BlockSpec((tm, n), lambda i: (i, 0)),
        out_shape=jax.ShapeDtypeStruct((m, n), F32),
        compiler_params=_cparams("arbitrary"),
    )(x, w)


def _proj_ln_kernel(*refs, n_lhs):
    a_refs = refs[:n_lhs]
    w_refs = refs[n_lhs:2 * n_lhs]
    x_ref, g_ref, b_ref, o_ref = refs[2 * n_lhs:]
    acc = _dot(a_refs[0][...].astype(BF16), w_refs[0][...])
    for a_ref, w_ref in zip(a_refs[1:], w_refs[1:]):
        acc = acc + _dot(a_ref[...].astype(BF16), w_ref[...])
    o_ref[...] = _layer_norm(DN_ALPHA * x_ref[...] + acc, g_ref[...], b_ref[...])


def _proj_ln(lhs, ws, x, g, b, tm):
    m, d = x.shape
    tm = min(tm, m)
    n_lhs = len(lhs)
    in_specs = [pl.BlockSpec((tm, a.shape[1]), lambda i: (i, 0)) for a in lhs]
    in_specs += [pl.BlockSpec(w.shape, lambda i: (0, 0)) for w in ws]
    in_specs += [pl.BlockSpec((tm, d), lambda i: (i, 0)),
                 pl.BlockSpec((1, d), lambda i: (0, 0)),
                 pl.BlockSpec((1, d), lambda i: (0, 0))]
    return pl.pallas_call(
        functools.partial(_proj_ln_kernel, n_lhs=n_lhs),
        grid=(m // tm,),
        in_specs=in_specs,
        out_specs=pl.BlockSpec((tm, d), lambda i: (i, 0)),
        out_shape=jax.ShapeDtypeStruct((m, d), F32),
        compiler_params=_cparams("arbitrary"),
    )(*lhs, *ws, x, g.reshape(1, d), b.reshape(1, d))


def _cumsum_rows(x, row):
    n = x.shape[0]
    s = 1
    while s < n:
        x = x + jnp.where(row >= s, pltpu.roll(x, s, axis=0), 0.0)
        s *= 2
    return x


def _hgrn_kernel(zq_ref, zf_ref, zi_ref, zg_ref, lb_ref, ng_ref, s0_ref,
                 o_ref, st_ref, st_scr, *, chunk, t_valid, n_chunks):
    c = pl.program_id(1)
    nb = chunk // SUBLANES

    @pl.when(c == 0)
    def _():
        st_scr[...] = s0_ref[0]

    row = lax.broadcasted_iota(jnp.int32, (chunk, HG_DK), 0)
    row8 = lax.broadcasted_iota(jnp.int32, (SUBLANES, HG_DK), 0)
    rowa = lax.broadcasted_iota(jnp.int32, (SUBLANES, chunk), 0)
    cola = lax.broadcasted_iota(jnp.int32, (SUBLANES, chunk), 1)

    for h in range(HG_HEADS):
        sl = slice(h * HG_DK, (h + 1) * HG_DK)
        zf = zf_ref[:, sl]
        zq = zq_ref[:, sl]
        zg = zg_ref[:, sl]
        v = zi_ref[:, sl]
        lb = lb_ref[:, sl]
        log_lb = jnp.log(jnp.maximum(lb, LB_FLOOR))
        log_sig = jnp.minimum(zf, 0.0) - jnp.log1p(jnp.exp(-jnp.abs(zf)))
        t = jnp.log1p(-lb) + log_sig
        logf = jnp.maximum(log_lb, t) + jnp.log1p(jnp.exp(-jnp.abs(log_lb - t)))
        k = -_expm1(logf)
        q = _silu(zq)
        if t_valid < chunk:
            valid = row < t_valid
            logf = jnp.where(valid, logf, 0.0)
            k = jnp.where(valid, k, 0.0)
        b = _cumsum_rows(logf, row)
        b_last = b[chunk - 1:chunk]
        st = st_scr[h]
        vb = v.astype(BF16)

        o = _dot_nt((q * jnp.exp(b)).astype(BF16), st.astype(BF16))
        blocks = []
        for i in range(nb):
            rs = slice(i * SUBLANES, (i + 1) * SUBLANES)
            qi, ki, bi = q[rs], k[rs], b[rs]
            if i > 0:
                r = b[i * SUBLANES - 1:i * SUBLANES]
                qt = qi * jnp.exp(bi - r)
                kt = k * jnp.exp(jnp.minimum(r - b, 0.0))
                att = _dot_nt(qt.astype(BF16), kt.astype(BF16))
                att = jnp.where(cola < i * SUBLANES, att, 0.0)
            else:
                att = jnp.zeros((SUBLANES, chunk), F32)
            for d in range(SUBLANES):
                if d == 0:
                    prod = qi * ki
                else:
                    kd = pltpu.roll(ki, d, axis=0)
                    bd = pltpu.roll(bi, d, axis=0)
                    dec = jnp.exp(jnp.where(row8 >= d, bi - bd, NEG_BIG))
                    prod = qi * kd * dec
                a = jnp.sum(prod, axis=1, keepdims=True)
                att = att + jnp.where(cola == rowa + (i * SUBLANES - d), a, 0.0)
            blocks.append(att)
        att = blocks[0] if nb == 1 else jnp.concatenate(blocks, axis=0)
        o = o + _dot(att.astype(BF16), vb)

        ke = k * jnp.exp(b_last - b)
        st_scr[h] = jnp.exp(b_last) * st + _dot_tn(vb, ke.astype(BF16))

        on = o * lax.rsqrt(jnp.mean(o * o, axis=-1, keepdims=True) + LN_EPS) * ng_ref[:, sl]
        o_ref[:, sl] = on * _silu(zg)

    @pl.when(c == n_chunks - 1)
    def _():
        st_ref[0] = st_scr[...]


def _hgrn(z, lb, norm_g, s0t, n_seq, t_len, chunk, t_valid):
    n_chunks = t_len // chunk
    zspec = lambda cb: pl.BlockSpec((chunk, HG_WIDTH), lambda b, c: (b * n_chunks + c, cb))
    pspec = pl.BlockSpec((1, HG_WIDTH), lambda b, c: (0, 0))
    sspec = pl.BlockSpec((1, HG_HEADS, HG_DK, HG_DK), lambda b, c: (b, 0, 0, 0))
    return pl.pallas_call(
        functools.partial(_hgrn_kernel, chunk=chunk, t_valid=t_valid, n_chunks=n_chunks),
        grid=(n_seq, n_chunks),
        in_specs=[zspec(0), zspec(1), zspec(2), zspec(3), pspec, pspec, sspec],
        out_specs=[pl.BlockSpec((chunk, HG_WIDTH), lambda b, c: (b * n_chunks + c, 0)), sspec],
        out_shape=[jax.ShapeDtypeStruct((n_seq * t_len, HG_WIDTH), F32),
                   jax.ShapeDtypeStruct(s0t.shape, F32)],
        scratch_shapes=[pltpu.VMEM((HG_HEADS, HG_DK, HG_DK), F32)],
        compiler_params=_cparams("arbitrary", "arbitrary"),
    )(z, z, z, z, lb.reshape(1, HG_WIDTH), norm_g.reshape(1, HG_WIDTH), s0t)


S5_LANE_CHUNK = 512


def _s5_kernel(u_ref, bre_ref, bim_ref, cre_ref, cim_ref, coef_ref, pre_ref, pim_ref,
               d_ref, wglu_ref, h0re_ref, h0im_ref,
               o_ref, hre_ref, him_ref, xr_scr, xi_scr, c_scr,
               *, tb, n_tb, per_group, t_valid):
    tstep = pl.program_id(1)
    u = u_ref[...]
    ub = u.astype(BF16)
    xr_raw = _dot(ub, bre_ref[...])
    xi_raw = _dot(ub, bim_ref[...])
    cfr = coef_ref[0:1]
    cfi = coef_ref[1:2]
    xr_scr[...] = cfr * xr_raw - cfi * xi_raw
    xi_scr[...] = cfr * xi_raw + cfi * xr_raw

    if not per_group:
        @pl.when(tstep == 0)
        def _():
            c_scr[0:1] = h0re_ref[0]
            c_scr[1:2] = h0im_ref[0]

    row = lax.broadcasted_iota(jnp.int32, (SUBLANES, S5_LANE_CHUNK), 0)
    for lc in range(S5_N // S5_LANE_CHUNK):
        ls = slice(lc * S5_LANE_CHUNK, (lc + 1) * S5_LANE_CHUNK)
        pr = pre_ref[:, ls]
        pi = pim_ref[:, ls]

        def body(g, carry, ls=ls, pr=pr, pi=pi):
            if per_group:
                cr = h0re_ref[g][:, ls]
                ci = h0im_ref[g][:, ls]
            else:
                cr, ci = carry
            rs = pl.ds(pl.multiple_of(g * SUBLANES, SUBLANES), SUBLANES)
            xr = xr_scr[rs, ls]
            xi = xi_scr[rs, ls]
            for s in (1, 2, 4):
                ar = pr[s - 1:s]
                ai = pi[s - 1:s]
                sr = jnp.where(row >= s, pltpu.roll(xr, s, axis=0), 0.0)
                si = jnp.where(row >= s, pltpu.roll(xi, s, axis=0), 0.0)
                xr, xi = xr + (ar * sr - ai * si), xi + (ar * si + ai * sr)
            hr = xr + (pr * cr - pi * ci)
            hi = xi + (pr * ci + pi * cr)
            xr_scr[rs, ls] = hr
            xi_scr[rs, ls] = hi
            if per_group:
                hre_ref[g, :, ls] = hr[t_valid - 1:t_valid]
                him_ref[g, :, ls] = hi[t_valid - 1:t_valid]
            return hr[SUBLANES - 1:SUBLANES], hi[SUBLANES - 1:SUBLANES]

        carry0 = (c_scr[0:1, ls], c_scr[1:2, ls])
        cr, ci = lax.fori_loop(0, tb // SUBLANES, body, carry0)
        if not per_group:
            c_scr[0:1, ls] = cr
            c_scr[1:2, ls] = ci

    y = _dot(xr_scr[...].astype(BF16), cre_ref[...]) - _dot(xi_scr[...].astype(BF16), cim_ref[...])
    y = y + d_ref[...] * u
    y = y * (0.5 * (1.0 + jnp.tanh(math.sqrt(2.0 / math.pi) * (y + 0.044715 * (y * y * y)))))
    gl = _dot(y.astype(BF16), wglu_ref[...])
    o_ref[...] = gl[:, :S5_WIDTH] * _sigmoid(gl[:, S5_WIDTH:])

    if not per_group:
        @pl.when(tstep == n_tb - 1)
        def _():
            hre_ref[0] = c_scr[0:1]
            him_ref[0] = c_scr[1:2]


def _s5(z, wts, h0re, h0im, n_seq, t_len, tb, per_group, t_valid):
    bre, bim, cre, cim, coef, pre, pim, dvec, wglu = wts
    if per_group:
        grid = (1, 1)
        n_tb = 1
        tb = z.shape[0]
        hspec = pl.BlockSpec(h0re.shape, lambda b, t: (0, 0, 0))
    else:
        n_tb = t_len // tb
        grid = (n_seq, n_tb)
        hspec = pl.BlockSpec((1, 1, S5_N), lambda b, t: (b, 0, 0))
    ucol = (4 * HG_WIDTH) // S5_WIDTH
    full = lambda a: pl.BlockSpec(a.shape, lambda b, t: (0,) * a.ndim)
    return pl.pallas_call(
        functools.partial(_s5_kernel, tb=tb, n_tb=n_tb, per_group=per_group, t_valid=t_valid),
        grid=grid,
        in_specs=[pl.BlockSpec((tb, S5_WIDTH), lambda b, t: (b * n_tb + t, ucol)),
                  full(bre), full(bim), full(cre), full(cim), full(coef), full(pre), full(pim),
                  full(dvec), full(wglu), hspec, hspec],
        out_specs=[pl.BlockSpec((tb, S5_WIDTH), lambda b, t: (b * n_tb + t, 0)), hspec, hspec],
        out_shape=[jax.ShapeDtypeStruct((z.shape[0], S5_WIDTH), F32),
                   jax.ShapeDtypeStruct(h0re.shape, F32),
                   jax.ShapeDtypeStruct(h0im.shape, F32)],
        scratch_shapes=[pltpu.VMEM((tb, S5_N), F32), pltpu.VMEM((tb, S5_N), F32),
                        pltpu.VMEM((2, S5_N), F32)],
        compiler_params=_cparams("arbitrary", "arbitrary"),
    )(z, bre, bim, cre, cim, coef, pre, pim, dvec, wglu, h0re, h0im)


def _s5_weights(lam_re, lam_im, log_step, b_re, b_im, c_re, c_im, d, w_glu):
    lr, li = lam_re.astype(F32), lam_im.astype(F32)
    dt = jnp.exp(log_step.astype(F32))[:, None]
    mag = jnp.exp(lr * dt)
    ab_re, ab_im = mag * jnp.cos(li * dt), mag * jnp.sin(li * dt)
    den = lr * lr + li * li
    nr, ni = ab_re - 1.0, ab_im
    coef_re = (nr * lr + ni * li) / den
    coef_im = (ni * lr - nr * li) / den
    ar, ai = ab_re.reshape(1, S5_N), ab_im.reshape(1, S5_N)
    pr, pi = [ar], [ai]
    for _ in range(SUBLANES - 1):
        pr, pi = pr + [pr[-1] * ar - pi[-1] * ai], pi + [pr[-1] * ai + pi[-1] * ar]
    pre = jnp.concatenate(pr, axis=0)
    pim = jnp.concatenate(pi, axis=0)
    coef = jnp.concatenate([coef_re.reshape(1, S5_N), coef_im.reshape(1, S5_N)], axis=0)
    eye = jnp.eye(S5_GROUPS, dtype=F32)
    bd = lambda w: jnp.einsum('gpc,gh->gchp', w, eye).reshape(S5_WIDTH, S5_N).astype(BF16)
    cd = lambda w: jnp.einsum('gcp,gh->gphc', w, eye).reshape(S5_N, S5_WIDTH).astype(BF16)
    return (bd(b_re), bd(b_im), cd(c_re), cd(c_im), coef, pre, pim,
            d.reshape(1, S5_WIDTH).astype(F32), w_glu.astype(BF16))


ALIBI_SPLIT_BITS = 7


def _qkv_kernel(*refs):
    x_ref, w_ref, wvt_ref = refs[:3]
    q_ref, k_ref, v_ref, vt_ref = refs[-4:]
    xb = x_ref[...].astype(BF16)
    z = _dot(xb, w_ref[...])
    q_ref[...] = z[:, :DA_QK]
    k_ref[0] = z[:, DA_QK:2 * DA_QK]
    v_ref[0] = z[:, 2 * DA_QK:]
    for slot in range(1, k_ref.shape[0]):
        k_ref[slot] = jnp.zeros(k_ref.shape[1:], F32)
        v_ref[slot] = jnp.zeros(v_ref.shape[1:], F32)
    vt_ref[...] = _dot_nt(wvt_ref[...], xb).astype(BF16)


def _qkv_proj(x, w, kv_bufs, layer, n_layers, tm):
    m, d = x.shape
    tm = min(tm, m)
    wvt = jnp.transpose(w[:, 2 * DA_QK:])
    in_specs = [pl.BlockSpec((tm, d), lambda i: (i, 0)), pl.BlockSpec(w.shape, lambda i: (0, 0)),
                pl.BlockSpec(wvt.shape, lambda i: (0, 0))]
    args = [x, w, wvt]
    aliases = {}
    if kv_bufs is not None:
        in_specs += [pl.BlockSpec(memory_space=pl.ANY)] * 2
        args += list(kv_bufs)
        aliases = {3: 1, 4: 2}
        kv_spec = pl.BlockSpec((1, tm, DA_QK), lambda i: (layer, i, 0))
    else:
        assert layer == 0
        kv_spec = pl.BlockSpec((n_layers, tm, DA_QK), lambda i: (0, i, 0))
    kv_shape = jax.ShapeDtypeStruct((n_layers, m, DA_QK), F32)
    return pl.pallas_call(
        _qkv_kernel,
        grid=(m // tm,),
        in_specs=in_specs,
        out_specs=[pl.BlockSpec((tm, DA_QK), lambda i: (i, 0)), kv_spec, kv_spec,
                   pl.BlockSpec((DA_HEADS * DA_DV, tm), lambda i: (0, i))],
        out_shape=[jax.ShapeDtypeStruct((m, DA_QK), F32), kv_shape, kv_shape,
                   jax.ShapeDtypeStruct((DA_HEADS * DA_DV, m), BF16)],
        input_output_aliases=aliases,
        compiler_params=_cparams("arbitrary"),
    )(*args)


def _pattn_kernel(qi_ref, ki_ref, q_ref, k_ref, vt_ref, slope_ref, lam_ref, g_ref, o_ref,
                  m0, l0, a0, m1, l1, a1, *, tq, lam_init):
    step = pl.program_id(2)
    qi = qi_ref[step]
    ki = ki_ref[step]
    stats = ((m0, l0, a0), (m1, l1, a1))

    @pl.when(ki == 0)
    def _():
        for m, l, a in stats:
            m[...] = jnp.full(m.shape, -jnp.inf, F32)
            l[...] = jnp.zeros(l.shape, F32)
            a[...] = jnp.zeros(a.shape, F32)

    q = q_ref[...] * (DA_DK ** -0.5)
    k = k_ref[0]
    lane = lax.broadcasted_iota(jnp.int32, q.shape, 1)
    kpos = ki * tq + lax.broadcasted_iota(jnp.int32, k.shape, 0)
    lo_mask = (1 << ALIBI_SPLIT_BITS) - 1
    slope = slope_ref[0]
    b_hi = slope * (kpos - jnp.bitwise_and(kpos, lo_mask)).astype(F32)
    b_lo = slope * jnp.bitwise_and(kpos, lo_mask).astype(F32)
    qs, ks = [], []
    for half in range(2):
        in_half = (lane >= DA_DK) if half else (lane < DA_DK)
        spare = 0 if half else DA_DK
        qs.append(jnp.where(in_half, q, jnp.where((lane == spare) | (lane == spare + 1), 1.0, 0.0))
                  .astype(BF16))
        ks.append(jnp.where(in_half, k, jnp.where(lane == spare, b_hi,
                                                  jnp.where(lane == spare + 1, b_lo, 0.0)))
                  .astype(BF16))
    vt = vt_ref[...]

    def update(masked):
        if masked:
            krow = lax.broadcasted_iota(jnp.int32, (tq, tq), 0)
            qcol = lax.broadcasted_iota(jnp.int32, (tq, tq), 1)
            keep = krow <= qcol
        for qh, kh, (m, l, a) in zip(qs, ks, stats):
            st = _dot_nt(kh, qh)
            if masked:
                st = jnp.where(keep, st, NEG_BIG)
            m_new = jnp.maximum(m[...], jnp.max(st, axis=0, keepdims=True))
            alpha = jnp.exp(m[...] - m_new)
            pt = jnp.exp(st - m_new)
            l[...] = alpha * l[...] + jnp.sum(pt, axis=0, keepdims=True)
            a[...] = alpha * a[...] + _dot(vt, pt.astype(BF16))
            m[...] = m_new

    @pl.when(ki < qi)
    def _():
        update(False)

    @pl.when(ki == qi)
    def _():
        update(True)
        ot = a0[...] / l0[...] - lam_ref[...] * (a1[...] / l1[...])
        ot = ot * lax.rsqrt(jnp.mean(ot * ot, axis=0, keepdims=True) + LN_EPS)
        o_ref[...] = jnp.transpose(ot) * (g_ref[...] * (1.0 - lam_init))


def _pattn(q, kbuf, vt, layer, slopes, lam, subln_g, n_seq, t_len, tq, lam_init):
    nq = t_len // tq
    pairs = [(i, j) for i in range(nq) for j in range(i + 1)]
    qi_tab = jnp.array([p[0] for p in pairs], jnp.int32)
    ki_tab = jnp.array([p[1] for p in pairs], jnp.int32)
    q_spec = pl.BlockSpec((tq, DA_DV), lambda b, h, s, qt, kt: (b * nq + qt[s], h))
    grid_spec = pltpu.PrefetchScalarGridSpec(
        num_scalar_prefetch=2,
        grid=(n_seq, DA_HEADS, len(pairs)),
        in_specs=[q_spec,
                  pl.BlockSpec((1, tq, DA_DV), lambda b, h, s, qt, kt: (layer, b * nq + kt[s], h)),
                  pl.BlockSpec((DA_DV, tq), lambda b, h, s, qt, kt: (h, b * nq + kt[s])),
                  pl.BlockSpec((1, 1, 1), lambda b, h, s, qt, kt: (h, 0, 0)),
                  pl.BlockSpec((1, 1), lambda b, h, s, qt, kt: (0, 0)),
                  pl.BlockSpec((1, DA_DV), lambda b, h, s, qt, kt: (0, 0))],
        out_specs=q_spec,
        scratch_shapes=[pltpu.VMEM((1, tq), F32), pltpu.VMEM((1, tq), F32), pltpu.VMEM((DA_DV, tq), F32)] * 2,
    )
    return pl.pallas_call(
        functools.partial(_pattn_kernel, tq=tq, lam_init=lam_init),
        grid_spec=grid_spec,
        out_shape=jax.ShapeDtypeStruct((n_seq * t_len, DA_HEADS * DA_DV), F32),
        compiler_params=_cparams("arbitrary", "arbitrary", "arbitrary"),
    )(qi_tab, ki_tab, q, kbuf, vt, slopes.reshape(DA_HEADS, 1, 1), lam.reshape(1, 1),
      subln_g.reshape(1, DA_DV))


MAX_PAGES_PER_STEP = 8


def _sattn_kernel(pt_ref, q_ref, kn_ref, vn_ref, cmat_ref, slope_ref, lam_ref, g_ref, *rest,
                  n_pages, pps, t_new, lam_init):
    kp_refs = rest[:pps]
    vp_refs = rest[pps:2 * pps]
    o_ref, m_scr, l_scr, a_scr = rest[2 * pps:]
    p = pl.program_id(1)
    nrow = 2 * t_new * DA_HEADS

    @pl.when(p == 0)
    def _():
        m_scr[...] = jnp.full(m_scr.shape, -jnp.inf, F32)
        l_scr[...] = jnp.zeros(l_scr.shape, F32)
        a_scr[...] = jnp.zeros(a_scr.shape, F32)

    q = q_ref[0] * (DA_DK ** -0.5)
    lane = lax.broadcasted_iota(jnp.int32, q.shape, 1)
    qrows = jnp.concatenate([jnp.where(lane < DA_DK, q, 0.0), jnp.where(lane >= DA_DK, q, 0.0)],
                            axis=0).astype(BF16)
    slope = slope_ref[...]

    def accumulate(scores, values):
        m_old = m_scr[...]
        m_new = m_old
        for s in scores:
            m_new = jnp.maximum(m_new, jnp.max(s, axis=1, keepdims=True))
        alpha = jnp.exp(m_old - m_new)
        l_new = alpha * l_scr[...]
        acc = alpha * a_scr[...]
        for s, vb in zip(scores, values):
            pr = jnp.exp(s - m_new)
            l_new = l_new + jnp.sum(pr, axis=1, keepdims=True)
            acc = acc + _dot(pr.astype(BF16), vb)
        m_scr[...] = m_new
        l_scr[...] = l_new
        a_scr[...] = acc

    cmat = cmat_ref[...]
    scores, values = [], []
    for i in range(pps):
        page_start = ((p * pps + i) * PAGE_SIZE).astype(F32)
        s = _dot_nt(qrows, kp_refs[i][0, 0].astype(BF16))
        scores.append(s + (cmat + slope * page_start))
        values.append(vp_refs[i][0, 0].astype(BF16))
    accumulate(scores, values)

    @pl.when(p == n_pages // pps - 1)
    def _():
        ncol = t_new * DA_HEADS
        r = lax.broadcasted_iota(jnp.int32, (nrow, ncol), 0)
        c = lax.broadcasted_iota(jnp.int32, (nrow, ncol), 1)
        hshift = DA_HEADS.bit_length() - 1
        dist = jnp.bitwise_and(jnp.right_shift(r, hshift), t_new - 1) - jnp.right_shift(c, hshift)
        ok = (jnp.bitwise_and(r, DA_HEADS - 1) == jnp.bitwise_and(c, DA_HEADS - 1)) & (dist >= 0)
        s = _dot_nt(qrows, kn_ref[0].astype(BF16))
        s = jnp.where(ok, s - slope * dist.astype(F32), NEG_BIG)
        accumulate([s], [vn_ref[0].astype(BF16)])
        on = a_scr[...] / l_scr[...]
        hr = nrow // 2
        o = on[:hr] - lam_ref[...] * on[hr:]
        o = o * lax.rsqrt(jnp.mean(o * o, axis=-1, keepdims=True) + LN_EPS) * g_ref[...]
        o_ref[0] = o * (1.0 - lam_init)


def _sattn(zq, zk, zv, cache_k, cache_v, page_table, layer, slopes, lam, subln_g, lam_init):
    n_seq, n_pages = page_table.shape
    pps = math.gcd(n_pages, MAX_PAGES_PER_STEP)
    hr = zq.shape[1]
    t_new = hr // DA_HEADS
    nrow = 2 * hr
    past = n_pages * PAGE_SIZE
    slope_col = jnp.tile(slopes, 2 * t_new).reshape(nrow, 1)
    r = jnp.arange(nrow)[:, None]
    c = jnp.arange(PAGE_SIZE * DA_HEADS)[None, :]
    dist0 = (past + (r // DA_HEADS) % t_new - c // DA_HEADS).astype(F32)
    cmat = jnp.where(r % DA_HEADS == c % DA_HEADS, -slope_col * dist0, NEG_BIG).astype(F32)
    seq_spec = pl.BlockSpec((1, hr, DA_DV), lambda s, p, pt: (s, 0, 0))
    const = lambda shape: pl.BlockSpec(shape, lambda s, p, pt: (0,) * len(shape))

    def page_spec(i):
        return pl.BlockSpec((1, 1, PAGE_SIZE * DA_HEADS, DA_DV),
                            lambda s, p, pt: (layer, pt[s * n_pages + p * pps + i], 0, 0))

    grid_spec = pltpu.PrefetchScalarGridSpec(
        num_scalar_prefetch=1,
        grid=(n_seq, n_pages // pps),
        in_specs=[seq_spec, seq_spec, seq_spec, const(cmat.shape), const((nrow, 1)), const((1, 1)),
                  const((1, DA_DV))] + [page_spec(i) for i in range(pps)] * 2,
        out_specs=seq_spec,
        scratch_shapes=[pltpu.VMEM((nrow, 1), F32), pltpu.VMEM((nrow, 1), F32),
                        pltpu.VMEM((nrow, DA_DV), F32)],
    )
    return pl.pallas_call(
        functools.partial(_sattn_kernel, n_pages=n_pages, pps=pps, t_new=t_new, lam_init=lam_init),
        grid_spec=grid_spec,
        out_shape=jax.ShapeDtypeStruct(zq.shape, F32),
        compiler_params=_cparams("arbitrary", "arbitrary"),
    )(page_table.reshape(-1), zq, zk, zv, cmat, slope_col, lam.reshape(1, 1),
      subln_g.reshape(1, DA_DV), *([cache_k] * pps), *([cache_v] * pps))


def _split_bf16(a):
    hi = a.astype(BF16)
    lo = (a - hi.astype(F32)).astype(BF16)
    return hi, lo


def _route(x, w, b):
    xh, xl = _split_bf16(x)
    wh, wl = _split_bf16(w)
    logits = _dot(xh, wh) + (_dot(xl, wh) + _dot(xh, wl)) + b
    lane = lax.broadcasted_iota(jnp.int32, logits.shape, 1).astype(F32)
    big = float(LANES)
    is_grp = lane < MOE_GROUPS
    gl = jnp.where(is_grp, logits, -jnp.inf)
    gmax = jnp.max(gl, axis=1, keepdims=True)
    gidx = jnp.min(jnp.where(gl == gmax, lane, big), axis=1, keepdims=True)
    den = jnp.sum(jnp.where(is_grp, jnp.exp(logits - gmax), 0.0), axis=1, keepdims=True)
    p_g = 1.0 / den
    lo = ROUTE_OFF + MOE_PER_GROUP * gidx
    in_grp = (lane >= lo) & (lane < lo + MOE_PER_GROUP)
    el = jnp.where(in_grp, logits, -jnp.inf)
    v1 = jnp.max(el, axis=1, keepdims=True)
    i1 = jnp.min(jnp.where(el == v1, lane, big), axis=1, keepdims=True)
    el2 = jnp.where(lane == i1, -jnp.inf, el)
    v2 = jnp.max(el2, axis=1, keepdims=True)
    i2 = jnp.min(jnp.where(el2 == v2, lane, big), axis=1, keepdims=True)
    e2 = jnp.exp(v2 - v1)
    w1 = (1.0 / (1.0 + e2)) * p_g
    w2 = (e2 / (1.0 + e2)) * p_g
    gates = jnp.where(lane == i1, w1, 0.0) + jnp.where(lane == i2, w2, 0.0)
    return gates, gidx, i1, i2


def _router_kernel(x_ref, w_ref, b_ref, g_ref):
    g_ref[...] = _route(x_ref[...], w_ref[...], b_ref[...])[0]


def _router(x, w_route, b_route, tm):
    m, d = x.shape
    tm = min(tm, m)
    return pl.pallas_call(
        _router_kernel,
        grid=(m // tm,),
        in_specs=[pl.BlockSpec((tm, d), lambda i: (i, 0)),
                  pl.BlockSpec((d, LANES), lambda i: (0, 0)),
                  pl.BlockSpec((1, LANES), lambda i: (0, 0))],
        out_specs=pl.BlockSpec((tm, LANES), lambda i: (i, 0)),
        out_shape=jax.ShapeDtypeStruct((m, LANES), F32),
        compiler_params=_cparams("arbitrary"),
    )(x, w_route, b_route)


def _moe_kernel(x_ref, g_ref, wg_ref, wu_ref, wd_ref, lg_ref, lb_ref, o_ref, xb_scr, y_scr):
    e = pl.program_id(1)

    @pl.when(e == 0)
    def _():
        xb_scr[...] = x_ref[...].astype(BF16)
        y_scr[...] = jnp.zeros(y_scr.shape, F32)

    lane = lax.broadcasted_iota(jnp.int32, g_ref.shape, 1)
    ge = jnp.sum(jnp.where(lane == e + ROUTE_OFF, g_ref[...], 0.0), axis=1, keepdims=True)
    xb = xb_scr[...]
    h = _silu(_dot(xb, wg_ref[0])) * _dot(xb, wu_ref[0]) * ge
    y_scr[...] += _dot(h.astype(BF16), wd_ref[0])

    @pl.when(e == MOE_EXPERTS - 1)
    def _():
        o_ref[...] = _layer_norm(DN_ALPHA * x_ref[...] + y_scr[...], lg_ref[...], lb_ref[...])


def _moe_ln(x, gates, wg, wu, wd, g, b, tm):
    m, d = x.shape
    tm = min(tm, m)
    return pl.pallas_call(
        _moe_kernel,
        grid=(m // tm, MOE_EXPERTS),
        in_specs=[pl.BlockSpec((tm, d), lambda i, e: (i, 0)),
                  pl.BlockSpec((tm, LANES), lambda i, e: (i, 0)),
                  pl.BlockSpec((1, d, MOE_FF), lambda i, e: (e, 0, 0)),
                  pl.BlockSpec((1, d, MOE_FF), lambda i, e: (e, 0, 0)),
                  pl.BlockSpec((1, MOE_FF, d), lambda i, e: (e, 0, 0)),
                  pl.BlockSpec((1, d), lambda i, e: (0, 0)),
                  pl.BlockSpec((1, d), lambda i, e: (0, 0))],
        out_specs=pl.BlockSpec((tm, d), lambda i, e: (i, 0)),
        out_shape=jax.ShapeDtypeStruct((m, d), F32),
        scratch_shapes=[pltpu.VMEM((tm, d), BF16), pltpu.VMEM((tm, d), F32)],
        compiler_params=_cparams("arbitrary", "arbitrary"),
    )(x, gates, wg, wu, wd, g.reshape(1, d), b.reshape(1, d))


MOE_PAIRS = [(a, b) for a in range(MOE_PER_GROUP) for b in range(a + 1, MOE_PER_GROUP)]
MOE_CLASSES = MOE_GROUPS * len(MOE_PAIRS)
CLASS_ELO = [g * MOE_PER_GROUP + a for g in range(MOE_GROUPS) for a, _ in MOE_PAIRS]
CLASS_EHI = [g * MOE_PER_GROUP + b for g in range(MOE_GROUPS) for _, b in MOE_PAIRS]
MOE_TILE = 256
PERM_CHUNK = 2048


def _router_sort_kernel(x_ref, w_ref, b_ref, cls_ref, rank_ref, cnt_ref, run_scr, *, n_steps):
    i = pl.program_id(0)

    @pl.when(i == 0)
    def _():
        run_scr[...] = jnp.zeros(run_scr.shape, F32)

    _, gidx, i1, i2 = _route(x_ref[...], w_ref[...], b_ref[...])
    base = ROUTE_OFF + MOE_PER_GROUP * gidx
    lo = jnp.minimum(i1, i2) - base
    hi = jnp.maximum(i1, i2) - base
    pair = jnp.where(lo == 0.0, hi - 1.0, jnp.where(lo == 1.0, hi + 1.0, float(len(MOE_PAIRS) - 1)))
    cls = gidx * float(len(MOE_PAIRS)) + pair
    tm = cls.shape[0]
    lane = lax.broadcasted_iota(jnp.int32, (tm, LANES), 1).astype(F32)
    onehot = jnp.where(lane == cls, 1.0, 0.0)
    r = lax.broadcasted_iota(jnp.int32, (tm, tm), 0)
    c = lax.broadcasted_iota(jnp.int32, (tm, tm), 1)
    earlier = jnp.where(c < r, 1.0, 0.0).astype(BF16)
    within = _dot(earlier, onehot.astype(BF16))
    rank = jnp.sum(onehot * (within + run_scr[...]), axis=1, keepdims=True)
    run_scr[...] += jnp.sum(onehot, axis=0, keepdims=True)
    cls_ref[...] = cls.astype(jnp.int32)
    rank_ref[...] = rank.astype(jnp.int32)

    @pl.when(i == n_steps - 1)
    def _():
        cnt_ref[...] = run_scr[...]


def _router_sort(x, w_route, b_route, tm):
    m, d = x.shape
    tm = min(tm, m)
    col = pl.BlockSpec((tm, 1), lambda i: (i, 0))
    return pl.pallas_call(
        functools.partial(_router_sort_kernel, n_steps=m // tm),
        grid=(m // tm,),
        in_specs=[pl.BlockSpec((tm, d), lambda i: (i, 0)),
                  pl.BlockSpec((d, LANES), lambda i: (0, 0)),
                  pl.BlockSpec((1, LANES), lambda i: (0, 0))],
        out_specs=[col, col, pl.BlockSpec((1, LANES), lambda i: (0, 0))],
        out_shape=[jax.ShapeDtypeStruct((m, 1), jnp.int32), jax.ShapeDtypeStruct((m, 1), jnp.int32),
                   jax.ShapeDtypeStruct((1, LANES), F32)],
        scratch_shapes=[pltpu.VMEM((1, LANES), F32)],
        compiler_params=_cparams("arbitrary"),
    )(x, w_route, b_route)


def _perm_kernel(idx_ref, src_ref, *rest, chunk, gather):
    dst_ref, sem = rest[-2:]
    base = pl.program_id(0) * chunk

    def row_copy(r):
        j = idx_ref[0, 0, r]
        if gather:
            return pltpu.make_async_copy(src_ref.at[pl.ds(j, 1)], dst_ref.at[pl.ds(base + r, 1)], sem)
        return pltpu.make_async_copy(src_ref.at[pl.ds(base + r, 1)], dst_ref.at[pl.ds(j, 1)], sem)

    def start(r, carry):
        row_copy(r).start()
        return carry

    def wait(r, carry):
        row_copy(r).wait()
        return carry

    lax.fori_loop(0, chunk, start, 0, unroll=8)
    lax.fori_loop(0, chunk, wait, 0, unroll=8)


def _permute_rows(src, idx, dst_init, gather):
    n = idx.shape[0]
    chunk = min(PERM_CHUNK, n)
    any_spec = pl.BlockSpec(memory_space=pl.ANY)
    args = [idx.reshape(n // chunk, 1, chunk), src]
    in_specs = [pl.BlockSpec((1, 1, chunk), lambda i: (i, 0, 0), memory_space=pltpu.SMEM), any_spec]
    aliases = {}
    if gather:
        out_shape = jax.ShapeDtypeStruct((n, src.shape[1]), src.dtype)
    else:
        args.append(dst_init)
        in_specs.append(any_spec)
        aliases = {2: 0}
        out_shape = jax.ShapeDtypeStruct(dst_init.shape, dst_init.dtype)
    return pl.pallas_call(
        functools.partial(_perm_kernel, chunk=chunk, gather=gather),
        grid=(n // chunk,),
        in_specs=in_specs,
        out_specs=any_spec,
        out_shape=out_shape,
        scratch_shapes=[pltpu.SemaphoreType.DMA(())],
        input_output_aliases=aliases,
        compiler_params=pltpu.CompilerParams(dimension_semantics=("arbitrary",),
                                             disable_bounds_checks=True),
    )(*args)


def _moe_tile_kernel(elo_ref, ehi_ref, ok_ref, x_ref, wr_ref, br_ref,
                     wg0, wu0, wd0, wg1, wu1, wd1, lg_ref, lb_ref, o_ref):
    t = pl.program_id(0)

    @pl.when(ok_ref[t] == 0)
    def _():
        o_ref[...] = jnp.zeros(o_ref.shape, F32)

    @pl.when(ok_ref[t] != 0)
    def _():
        x = x_ref[...]
        gates = _route(x, wr_ref[...], br_ref[...])[0]
        lane = lax.broadcasted_iota(jnp.int32, gates.shape, 1)
        xb = x.astype(BF16)
        y = None
        for e, wg, wu, wd in ((elo_ref[t], wg0, wu0, wd0), (ehi_ref[t], wg1, wu1, wd1)):
            ge = jnp.sum(jnp.where(lane == e + ROUTE_OFF, gates, 0.0), axis=1, keepdims=True)
            h = _silu(_dot(xb, wg[0])) * _dot(xb, wu[0]) * ge
            ye = _dot(h.astype(BF16), wd[0])
            y = ye if y is None else y + ye
        o_ref[...] = _layer_norm(DN_ALPHA * x + y, lg_ref[...], lb_ref[...])


def _moe_sorted_ln(x, w_route, b_route, wg, wu, wd, g, b, tm_router):
    n, d = x.shape
    tile = MOE_TILE
    n_tiles = n // tile + MOE_CLASSES
    cls, rank, cnt = _router_sort(x, w_route, b_route, tm_router)
    counts = cnt[0, :MOE_CLASSES].astype(jnp.int32)
    tiles_c = (counts + tile - 1) // tile
    tile_end = jnp.cumsum(tiles_c)
    tile_start = tile_end - tiles_c
    dest = jnp.take(tile_start, cls[:, 0]) * tile + rank[:, 0]
    tids = jnp.arange(n_tiles, dtype=jnp.int32)
    n_used = tile_end[-1]
    t_cls = jnp.searchsorted(tile_end, jnp.minimum(tids, n_used - 1), side='right').astype(jnp.int32)
    t_elo = jnp.take(jnp.array(CLASS_ELO, jnp.int32), t_cls)
    t_ehi = jnp.take(jnp.array(CLASS_EHI, jnp.int32), t_cls)
    t_ok = (tids < n_used).astype(jnp.int32)

    xs = _permute_rows(x, dest, jnp.zeros((n_tiles * tile, d), x.dtype), gather=False)
    wspec = lambda tab_pos: pl.BlockSpec(
        (1,) + wg.shape[1:], lambda t, elo, ehi, ok: ((elo, ehi)[tab_pos][t], 0, 0))
    dspec = lambda tab_pos: pl.BlockSpec(
        (1,) + wd.shape[1:], lambda t, elo, ehi, ok: ((elo, ehi)[tab_pos][t], 0, 0))
    const = lambda shape: pl.BlockSpec(shape, lambda t, elo, ehi, ok: (0,) * len(shape))
    row_spec = pl.BlockSpec((tile, d), lambda t, elo, ehi, ok: (t, 0))
    grid_spec = pltpu.PrefetchScalarGridSpec(
        num_scalar_prefetch=3,
        grid=(n_tiles,),
        in_specs=[row_spec, const(w_route.shape), const(b_route.shape),
                  wspec(0), wspec(0), dspec(0), wspec(1), wspec(1), dspec(1),
                  const((1, d)), const((1, d))],
        out_specs=row_spec,
    )
    ys = pl.pallas_call(
        _moe_tile_kernel,
        grid_spec=grid_spec,
        out_shape=jax.ShapeDtypeStruct(xs.shape, F32),
        compiler_params=_cparams("arbitrary"),
    )(t_elo, t_ehi, t_ok, xs, w_route, b_route, wg, wu, wd, wg, wu, wd, g.reshape(1, d), b.reshape(1, d))
    return _permute_rows(ys, dest, None, gather=True)


def _pad_time(a, n_seq, t_len, t_pad):
    w = a.shape[1]
    a = a.reshape(n_seq, t_len, w)
    a = jnp.pad(a, ((0, 0), (0, t_pad - t_len), (0, 0)))
    return a.reshape(n_seq * t_pad, w)


def _forward(x_prompt, x_sample, cache_k, cache_v, state_hgrn, state_s5_re, state_s5_im, page_table,
             w_in_ab, w_out_ab, hg_lb_param, hg_norm_g, s5_lam_re, s5_lam_im, s5_log_step,
             s5_b_re, s5_b_im, s5_c_re, s5_c_im, s5_d, s5_w_glu,
             w_in_c, w_out_c, da_lam_q1, da_lam_k1, da_lam_q2, da_lam_k2, da_subln_g,
             moe_w_group, moe_b_group, moe_w_expert, moe_b_expert, moe_w_gate, moe_w_up, moe_w_down,
             ln_g, ln_b):
    bp, tp, d = x_prompt.shape
    bs, ts, _ = x_sample.shape
    ts_pad = SUBLANES
    n_pool = cache_k.shape[1]
    xp = x_prompt.reshape(bp * tp, d)
    xs = x_sample.reshape(bs * ts, d)
    tm_p = min(512, bp * tp)
    tq = min(512, tp)
    s5_tb = min(256, tp)

    slopes = 2.0 ** (-8.0 * jnp.arange(1, DA_HEADS + 1, dtype=F32) / DA_HEADS)
    sm = jax.nn.softmax(hg_lb_param.astype(F32), axis=0)
    lbs = jnp.cumsum(sm, axis=0) - sm[:1]
    ck = cache_k.reshape(cache_k.shape[0], n_pool, PAGE_SIZE * DA_HEADS, 2 * DA_DK)
    cv = cache_v.reshape(cache_v.shape[0], n_pool, PAGE_SIZE * DA_HEADS, DA_DV)
    zeros_hg = jnp.zeros((bp, HG_HEADS, HG_DK, HG_DK), F32)
    zeros_s5 = jnp.zeros((bp, 1, S5_N), F32)

    new_hg_p, new_re_p, new_im_p = [], [], []
    kv_bufs_p = None
    new_k_s, new_v_s, new_hg_s, new_re_s, new_im_s = [], [], [], [], []
    for i in range(DEPTH):
        j = i // 2
        if i % 2 == 0:
            w_in = w_in_ab[j].astype(BF16)
            w_out = w_out_ab[j].astype(BF16)
            s5w = _s5_weights(s5_lam_re[j], s5_lam_im[j], s5_log_step[j], s5_b_re[j], s5_b_im[j],
                              s5_c_re[j], s5_c_im[j], s5_d[j], s5_w_glu[j])
            zp = _matmul(xp, w_in, tm_p)
            oa_p, hg_p = _hgrn(zp, lbs[j], hg_norm_g[j], zeros_hg, bp, tp, math.gcd(tp, HG_CHUNK), tp)
            ob_p, re_p, im_p = _s5(zp, s5w, zeros_s5, zeros_s5, bp, tp, s5_tb, False, SUBLANES)
            xp = _proj_ln([oa_p, ob_p], [w_out[:HG_WIDTH], w_out[HG_WIDTH:]], xp,
                          ln_g[i, 0], ln_b[i, 0], tm_p)
            zs = _pad_time(_matmul(xs, w_in, bs * ts), bs, ts, ts_pad)
            s0t = jnp.swapaxes(state_hgrn[j], -1, -2)
            oa_s, hg_s = _hgrn(zs, lbs[j], hg_norm_g[j], s0t, bs, ts_pad, ts_pad, ts)
            ob_s, re_s, im_s = _s5(zs, s5w, state_s5_re[j].reshape(bs, 1, S5_N),
                                   state_s5_im[j].reshape(bs, 1, S5_N), bs, ts_pad, ts_pad, True, ts)
            unpad = lambda a: a.reshape(bs, ts_pad, -1)[:, :ts].reshape(bs * ts, -1)
            xs = _proj_ln([unpad(oa_s), unpad(ob_s)], [w_out[:HG_WIDTH], w_out[HG_WIDTH:]], xs,
                          ln_g[i, 0], ln_b[i, 0], bs * ts)
            new_hg_p.append(jnp.swapaxes(hg_p, -1, -2))
            new_hg_s.append(jnp.swapaxes(hg_s, -1, -2))
            new_re_p.append(re_p.reshape(bp, S5_GROUPS, S5_STATE))
            new_im_p.append(im_p.reshape(bp, S5_GROUPS, S5_STATE))
            new_re_s.append(re_s.reshape(bs, S5_GROUPS, S5_STATE))
            new_im_s.append(im_s.reshape(bs, S5_GROUPS, S5_STATE))
        else:
            w_in = w_in_c[j].astype(BF16)
            w_out = w_out_c[j].astype(BF16)
            lam_init = 0.8 - 0.6 * math.exp(-0.3 * i)
            lam = (jnp.exp(jnp.sum(da_lam_q1[j].astype(F32) * da_lam_k1[j].astype(F32)))
                   - jnp.exp(jnp.sum(da_lam_q2[j].astype(F32) * da_lam_k2[j].astype(F32))) + lam_init)
            qp, kbuf_p, vbuf_p, vt_p = _qkv_proj(xp, w_in, kv_bufs_p, j, DEPTH // 2, tm_p // 2)
            kv_bufs_p = (kbuf_p, vbuf_p)
            oc_p = _pattn(qp, kbuf_p, vt_p, j, slopes, lam, da_subln_g[j], bp, tp, tq, lam_init)
            xp = _proj_ln([oc_p], [w_out], xp, ln_g[i, 0], ln_b[i, 0], tm_p)

            zs = _matmul(xs, w_in, bs * ts)
            rows = lambda a: a.reshape(bs, ts * DA_HEADS, DA_DV)
            zq_s, zk_s, zv_s = zs[:, :DA_QK], zs[:, DA_QK:2 * DA_QK], zs[:, 2 * DA_QK:]
            oc_s = _sattn(rows(zq_s), rows(zk_s), rows(zv_s), ck, cv, page_table, j,
                          slopes, lam, da_subln_g[j], lam_init)
            new_k_s.append(zk_s.reshape(bs, ts, DA_HEADS, 2 * DA_DK))
            new_v_s.append(zv_s.reshape(bs, ts, DA_HEADS, DA_DV))
            xs = _proj_ln([oc_s.reshape(bs * ts, DA_HEADS * DA_DV)], [w_out], xs,
                          ln_g[i, 0], ln_b[i, 0], bs * ts)

        w_route = jnp.pad(jnp.concatenate([moe_w_group[i], moe_w_expert[i]], axis=1).astype(F32),
                          ((0, 0), (0, LANES - MOE_GROUPS - MOE_EXPERTS)))
        b_route = jnp.pad(jnp.concatenate([moe_b_group[i], moe_b_expert[i]]).astype(F32),
                          (0, LANES - MOE_GROUPS - MOE_EXPERTS)).reshape(1, LANES)
        wg = moe_w_gate[i].astype(BF16)
        wu = moe_w_up[i].astype(BF16)
        wd = moe_w_down[i].astype(BF16)
        xp = _moe_sorted_ln(xp, w_route, b_route, wg, wu, wd, ln_g[i, 1], ln_b[i, 1], tm_p)
        xs = _moe_ln(xs, _router(xs, w_route, b_route, bs * ts), wg, wu, wd,
                     ln_g[i, 1], ln_b[i, 1], bs * ts)

    return (xp.reshape(bp, tp, d), xs.reshape(bs, ts, d),
            kv_bufs_p[0].reshape(DEPTH // 2, bp, tp, DA_HEADS, 2 * DA_DK),
            kv_bufs_p[1].reshape(DEPTH // 2, bp, tp, DA_HEADS, DA_DV), jnp.stack(new_hg_p),
            jnp.stack(new_re_p), jnp.stack(new_im_p),
            jnp.stack(new_k_s), jnp.stack(new_v_s), jnp.stack(new_hg_s),
            jnp.stack(new_re_s), jnp.stack(new_im_s))


_forward_jit = jax.jit(_forward)


def kernel(x_prompt, x_sample, cache_k, cache_v, state_hgrn, state_s5_re, state_s5_im, page_table, w_in_ab, w_out_ab, hg_lb_param, hg_norm_g, s5_lam_re, s5_lam_im, s5_log_step, s5_b_re, s5_b_im, s5_c_re, s5_c_im, s5_d, s5_w_glu, w_in_c, w_out_c, da_lam_q1, da_lam_k1, da_lam_q2, da_lam_k2, da_subln_g, moe_w_group, moe_b_group, moe_w_expert, moe_b_expert, moe_w_gate, moe_w_up, moe_w_down, ln_g, ln_b):
    return _forward_jit(x_prompt, x_sample, cache_k, cache_v, state_hgrn, state_s5_re, state_s5_im, page_table, w_in_ab, w_out_ab, hg_lb_param, hg_norm_g, s5_lam_re, s5_lam_im, s5_log_step, s5_b_re, s5_b_im, s5_c_re, s5_c_im, s5_d, s5_w_glu, w_in_c, w_out_c, da_lam_q1, da_lam_k1, da_lam_q2, da_lam_k2, da_subln_g, moe_w_group, moe_b_group, moe_w_expert, moe_b_expert, moe_w_gate, moe_w_up, moe_w_down, ln_g, ln_b)
```

```python
import functools
import math

import jax
import jax.numpy as jnp
from jax import lax
from jax.experimental import pallas as pl
from jax.experimental.pallas import tpu as pltpu

F32 = jnp.float32
BF16 = jnp.bfloat16

D_MODEL = 1024
DEPTH = 4
HG_HEADS = 4
HG_DK = 128
HG_WIDTH = HG_HEADS * HG_DK
S5_WIDTH = 512
S5_GROUPS = 32
S5_GROUP = 16
S5_STATE = 64
S5_N = S5_GROUPS * S5_STATE
DA_HEADS = 8
DA_DK = 64
DA_DV = 128
DA_QK = DA_HEADS * 2 * DA_DK
MOE_GROUPS = 4
MOE_PER_GROUP = 4
MOE_EXPERTS = 16
MOE_FF = 512
PAGE_SIZE = 128
DN_ALPHA = (2.0 * DEPTH) ** 0.25
LN_EPS = 1e-5
NEG_BIG = -1e30
LB_FLOOR = 1e-30

SUBLANES = 8
LANES = 128
VMEM_LIMIT = 48 * 1024 * 1024

HG_CHUNK = 64
ROUTE_OFF = MOE_GROUPS


def _cparams(*sem):
    return pltpu.CompilerParams(dimension_semantics=sem, vmem_limit_bytes=VMEM_LIMIT)


def _dot(a, b):
    return jnp.dot(a, b, preferred_element_type=F32)


def _dot_nt(a, b):
    return lax.dot_general(a, b, (((1,), (1,)), ((), ())), preferred_element_type=F32)


def _dot_tn(a, b):
    return lax.dot_general(a, b, (((0,), (0,)), ((), ())), preferred_element_type=F32)


def _sigmoid(x):
    return 1.0 / (1.0 + jnp.exp(-x))


def _silu(x):
    return x * _sigmoid(x)


def _expm1(x):
    u = jnp.exp(x)
    um1 = u - 1.0
    r = jnp.where(u == 1.0, x, um1 * x / jnp.log(u))
    return jnp.where(um1 == -1.0, -1.0, r)


def _layer_norm(y, g, b):
    mu = jnp.mean(y, axis=-1, keepdims=True)
    yc = y - mu
    var = jnp.mean(yc * yc, axis=-1, keepdims=True)
    return yc * lax.rsqrt(var + LN_EPS) * g + b


def _mm_kernel(x_ref, w_ref, o_ref):
    o_ref[...] = _dot(x_ref[...].astype(BF16), w_ref[...])


def _matmul(x, w, tm):
    m, k = x.shape
    n = w.shape[1]
    tm = min(tm, m)
    return pl.pallas_call(
        _mm_kernel,
        grid=(m // tm,),
        in_specs=[pl.BlockSpec((tm, k), lambda i: (i, 0)),
                  pl.BlockSpec((k, n), lambda i: (0, 0))],
        out_specs=pl.BlockSpec((tm, n), lambda i: (i, 0)),
        out_shape=jax.ShapeDtypeStruct((m, n), F32),
        compiler_params=_cparams("arbitrary"),
    )(x, w)


def _proj_ln_kernel(*refs, n_lhs):
    a_refs = refs[:n_lhs]
    w_refs = refs[n_lhs:2 * n_lhs]
    x_ref, g_ref, b_ref, o_ref = refs[2 * n_lhs:]
    acc = _dot(a_refs[0][...].astype(BF16), w_refs[0][...])
    for a_ref, w_ref in zip(a_refs[1:], w_refs[1:]):
        acc = acc + _dot(a_ref[...].astype(BF16), w_ref[...])
    o_ref[...] = _layer_norm(DN_ALPHA * x_ref[...] + acc, g_ref[...], b_ref[...])


def _proj_ln(lhs, ws, x, g, b, tm):
    m, d = x.shape
    tm = min(tm, m)
    n_lhs = len(lhs)
    in_specs = [pl.BlockSpec((tm, a.shape[1]), lambda i: (i, 0)) for a in lhs]
    in_specs += [pl.BlockSpec(w.shape, lambda i: (0, 0)) for w in ws]
    in_specs += [pl.BlockSpec((tm, d), lambda i: (i, 0)),
                 pl.BlockSpec((1, d), lambda i: (0, 0)),
                 pl.BlockSpec((1, d), lambda i: (0, 0))]
    return pl.pallas_call(
        functools.partial(_proj_ln_kernel, n_lhs=n_lhs),
        grid=(m // tm,),
        in_specs=in_specs,
        out_specs=pl.BlockSpec((tm, d), lambda i: (i, 0)),
        out_shape=jax.ShapeDtypeStruct((m, d), F32),
        compiler_params=_cparams("arbitrary"),
    )(*lhs, *ws, x, g.reshape(1, d), b.reshape(1, d))


def _cumsum_rows(x, row):
    n = x.shape[0]
    s = 1
    while s < n:
        x = x + jnp.where(row >= s, pltpu.roll(x, s, axis=0), 0.0)
        s *= 2
    return x


def _hgrn_kernel(zq_ref, zf_ref, zi_ref, zg_ref, lb_ref, ng_ref, s0_ref,
                 o_ref, st_ref, st_scr, *, chunk, t_valid, n_chunks):
    c = pl.program_id(1)
    nb = chunk // SUBLANES

    @pl.when(c == 0)
    def _():
        st_scr[...] = s0_ref[0]

    row = lax.broadcasted_iota(jnp.int32, (chunk, HG_DK), 0)
    row8 = lax.broadcasted_iota(jnp.int32, (SUBLANES, HG_DK), 0)
    rowa = lax.broadcasted_iota(jnp.int32, (SUBLANES, chunk), 0)
    cola = lax.broadcasted_iota(jnp.int32, (SUBLANES, chunk), 1)

    for h in range(HG_HEADS):
        sl = slice(h * HG_DK, (h + 1) * HG_DK)
        zf = zf_ref[:, sl]
        zq = zq_ref[:, sl]
        zg = zg_ref[:, sl]
        v = zi_ref[:, sl]
        lb = lb_ref[:, sl]
        log_lb = jnp.log(jnp.maximum(lb, LB_FLOOR))
        log_sig = jnp.minimum(zf, 0.0) - jnp.log1p(jnp.exp(-jnp.abs(zf)))
        t = jnp.log1p(-lb) + log_sig
        logf = jnp.maximum(log_lb, t) + jnp.log1p(jnp.exp(-jnp.abs(log_lb - t)))
        k = -_expm1(logf)
        q = _silu(zq)
        if t_valid < chunk:
            valid = row < t_valid
            logf = jnp.where(valid, logf, 0.0)
            k = jnp.where(valid, k, 0.0)
        b = _cumsum_rows(logf, row)
        b_last = b[chunk - 1:chunk]
        st = st_scr[h]
        vb = v.astype(BF16)

        o = _dot_nt((q * jnp.exp(b)).astype(BF16), st.astype(BF16))
        blocks = []
        for i in range(nb):
            rs = slice(i * SUBLANES, (i + 1) * SUBLANES)
            qi, ki, bi = q[rs], k[rs], b[rs]
            if i > 0:
                r = b[i * SUBLANES - 1:i * SUBLANES]
                qt = qi * jnp.exp(bi - r)
                kt = k * jnp.exp(jnp.minimum(r - b, 0.0))
                att = _dot_nt(qt.astype(BF16), kt.astype(BF16))
                att = jnp.where(cola < i * SUBLANES, att, 0.0)
            else:
                att = jnp.zeros((SUBLANES, chunk), F32)
            for d in range(SUBLANES):
                if d == 0:
                    prod = qi * ki
                else:
                    kd = pltpu.roll(ki, d, axis=0)
                    bd = pltpu.roll(bi, d, axis=0)
                    dec = jnp.exp(jnp.where(row8 >= d, bi - bd, NEG_BIG))
                    prod = qi * kd * dec
                a = jnp.sum(prod, axis=1, keepdims=True)
                att = att + jnp.where(cola == rowa + (i * SUBLANES - d), a, 0.0)
            blocks.append(att)
        att = blocks[0] if nb == 1 else jnp.concatenate(blocks, axis=0)
        o = o + _dot(att.astype(BF16), vb)

        ke = k * jnp.exp(b_last - b)
        st_scr[h] = jnp.exp(b_last) * st + _dot_tn(vb, ke.astype(BF16))

        on = o * lax.rsqrt(jnp.mean(o * o, axis=-1, keepdims=True) + LN_EPS) * ng_ref[:, sl]
        o_ref[:, sl] = on * _silu(zg)

    @pl.when(c == n_chunks - 1)
    def _():
        st_ref[0] = st_scr[...]


def _hgrn(z, lb, norm_g, s0t, n_seq, t_len, chunk, t_valid):
    n_chunks = t_len // chunk
    zspec = lambda cb: pl.BlockSpec((chunk, HG_WIDTH), lambda b, c: (b * n_chunks + c, cb))
    pspec = pl.BlockSpec((1, HG_WIDTH), lambda b, c: (0, 0))
    sspec = pl.BlockSpec((1, HG_HEADS, HG_DK, HG_DK), lambda b, c: (b, 0, 0, 0))
    return pl.pallas_call(
        functools.partial(_hgrn_kernel, chunk=chunk, t_valid=t_valid, n_chunks=n_chunks),
        grid=(n_seq, n_chunks),
        in_specs=[zspec(0), zspec(1), zspec(2), zspec(3), pspec, pspec, sspec],
        out_specs=[pl.BlockSpec((chunk, HG_WIDTH), lambda b, c: (b * n_chunks + c, 0)), sspec],
        out_shape=[jax.ShapeDtypeStruct((n_seq * t_len, HG_WIDTH), F32),
                   jax.ShapeDtypeStruct(s0t.shape, F32)],
        scratch_shapes=[pltpu.VMEM((HG_HEADS, HG_DK, HG_DK), F32)],
        compiler_params=_cparams("arbitrary", "arbitrary"),
    )(z, z, z, z, lb.reshape(1, HG_WIDTH), norm_g.reshape(1, HG_WIDTH), s0t)


S5_LANE_CHUNK = 512


def _s5_kernel(u_ref, bre_ref, bim_ref, cre_ref, cim_ref, coef_ref, pre_ref, pim_ref,
               d_ref, wglu_ref, h0re_ref, h0im_ref,
               o_ref, hre_ref, him_ref, xr_scr, xi_scr, c_scr,
               *, tb, n_tb, per_group, t_valid):
    tstep = pl.program_id(1)
    u = u_ref[...]
    ub = u.astype(BF16)
    xr_raw = _dot(ub, bre_ref[...])
    xi_raw = _dot(ub, bim_ref[...])
    cfr = coef_ref[0:1]
    cfi = coef_ref[1:2]
    xr_scr[...] = cfr * xr_raw - cfi * xi_raw
    xi_scr[...] = cfr * xi_raw + cfi * xr_raw

    if not per_group:
        @pl.when(tstep == 0)
        def _():
            c_scr[0:1] = h0re_ref[0]
            c_scr[1:2] = h0im_ref[0]

    row = lax.broadcasted_iota(jnp.int32, (SUBLANES, S5_LANE_CHUNK), 0)
    for lc in range(S5_N // S5_LANE_CHUNK):
        ls = slice(lc * S5_LANE_CHUNK, (lc + 1) * S5_LANE_CHUNK)
        pr = pre_ref[:, ls]
        pi = pim_ref[:, ls]

        def body(g, carry, ls=ls, pr=pr, pi=pi):
            if per_group:
                cr = h0re_ref[g][:, ls]
                ci = h0im_ref[g][:, ls]
            else:
                cr, ci = carry
            rs = pl.ds(pl.multiple_of(g * SUBLANES, SUBLANES), SUBLANES)
            xr = xr_scr[rs, ls]
            xi = xi_scr[rs, ls]
            for s in (1, 2, 4):
                ar = pr[s - 1:s]
                ai = pi[s - 1:s]
                sr = jnp.where(row >= s, pltpu.roll(xr, s, axis=0), 0.0)
                si = jnp.where(row >= s, pltpu.roll(xi, s, axis=0), 0.0)
                xr, xi = xr + (ar * sr - ai * si), xi + (ar * si + ai * sr)
            hr = xr + (pr * cr - pi * ci)
            hi = xi + (pr * ci + pi * cr)
            xr_scr[rs, ls] = hr
            xi_scr[rs, ls] = hi
            if per_group:
                hre_ref[g, :, ls] = hr[t_valid - 1:t_valid]
                him_ref[g, :, ls] = hi[t_valid - 1:t_valid]
            return hr[SUBLANES - 1:SUBLANES], hi[SUBLANES - 1:SUBLANES]

        carry0 = (c_scr[0:1, ls], c_scr[1:2, ls])
        cr, ci = lax.fori_loop(0, tb // SUBLANES, body, carry0)
        if not per_group:
            c_scr[0:1, ls] = cr
            c_scr[1:2, ls] = ci

    y = _dot(xr_scr[...].astype(BF16), cre_ref[...]) - _dot(xi_scr[...].astype(BF16), cim_ref[...])
    y = y + d_ref[...] * u
    y = y * (0.5 * (1.0 + jnp.tanh(math.sqrt(2.0 / math.pi) * (y + 0.044715 * (y * y * y)))))
    gl = _dot(y.astype(BF16), wglu_ref[...])
    o_ref[...] = gl[:, :S5_WIDTH] * _sigmoid(gl[:, S5_WIDTH:])

    if not per_group:
        @pl.when(tstep == n_tb - 1)
        def _():
            hre_ref[0] = c_scr[0:1]
            him_ref[0] = c_scr[1:2]


def _s5(z, wts, h0re, h0im, n_seq, t_len, tb, per_group, t_valid):
    bre, bim, cre, cim, coef, pre, pim, dvec, wglu = wts
    if per_group:
        grid = (1, 1)
        n_tb = 1
        tb = z.shape[0]
        hspec = pl.BlockSpec(h0re.shape, lambda b, t: (0, 0, 0))
    else:
        n_tb = t_len // tb
        grid = (n_seq, n_tb)
        hspec = pl.BlockSpec((1, 1, S5_N), lambda b, t: (b, 0, 0))
    ucol = (4 * HG_WIDTH) // S5_WIDTH
    full = lambda a: pl.BlockSpec(a.shape, lambda b, t: (0,) * a.ndim)
    return pl.pallas_call(
        functools.partial(_s5_kernel, tb=tb, n_tb=n_tb, per_group=per_group, t_valid=t_valid),
        grid=grid,
        in_specs=[pl.BlockSpec((tb, S5_WIDTH), lambda b, t: (b * n_tb + t, ucol)),
                  full(bre), full(bim), full(cre), full(cim), full(coef), full(pre), full(pim),
                  full(dvec), full(wglu), hspec, hspec],
        out_specs=[pl.BlockSpec((tb, S5_WIDTH), lambda b, t: (b * n_tb + t, 0)), hspec, hspec],
        out_shape=[jax.ShapeDtypeStruct((z.shape[0], S5_WIDTH), F32),
                   jax.ShapeDtypeStruct(h0re.shape, F32),
                   jax.ShapeDtypeStruct(h0im.shape, F32)],
        scratch_shapes=[pltpu.VMEM((tb, S5_N), F32), pltpu.VMEM((tb, S5_N), F32),
                        pltpu.VMEM((2, S5_N), F32)],
        compiler_params=_cparams("arbitrary", "arbitrary"),
    )(z, bre, bim, cre, cim, coef, pre, pim, dvec, wglu, h0re, h0im)


def _s5_weights(lam_re, lam_im, log_step, b_re, b_im, c_re, c_im, d, w_glu):
    lr, li = lam_re.astype(F32), lam_im.astype(F32)
    dt = jnp.exp(log_step.astype(F32))[:, None]
    mag = jnp.exp(lr * dt)
    ab_re, ab_im = mag * jnp.cos(li * dt), mag * jnp.sin(li * dt)
    den = lr * lr + li * li
    nr, ni = ab_re - 1.0, ab_im
    coef_re = (nr * lr + ni * li) / den
    coef_im = (ni * lr - nr * li) / den
    ar, ai = ab_re.reshape(1, S5_N), ab_im.reshape(1, S5_N)
    pr, pi = [ar], [ai]
    for _ in range(SUBLANES - 1):
        pr, pi = pr + [pr[-1] * ar - pi[-1] * ai], pi + [pr[-1] * ai + pi[-1] * ar]
    pre = jnp.concatenate(pr, axis=0)
    pim = jnp.concatenate(pi, axis=0)
    coef = jnp.concatenate([coef_re.reshape(1, S5_N), coef_im.reshape(1, S5_N)], axis=0)
    eye = jnp.eye(S5_GROUPS, dtype=F32)
    bd = lambda w: jnp.einsum('gpc,gh->gchp', w, eye).reshape(S5_WIDTH, S5_N).astype(BF16)
    cd = lambda w: jnp.einsum('gcp,gh->gphc', w, eye).reshape(S5_N, S5_WIDTH).astype(BF16)
    return (bd(b_re), bd(b_im), cd(c_re), cd(c_im), coef, pre, pim,
            d.reshape(1, S5_WIDTH).astype(F32), w_glu.astype(BF16))


ALIBI_SPLIT_BITS = 7


def _qkv_kernel(*refs):
    x_ref, w_ref, wvt_ref = refs[:3]
    q_ref, k_ref, v_ref, vt_ref = refs[-4:]
    xb = x_ref[...].astype(BF16)
    z = _dot(xb, w_ref[...])
    q_ref[...] = z[:, :DA_QK]
    k_ref[0] = z[:, DA_QK:2 * DA_QK]
    v_ref[0] = z[:, 2 * DA_QK:]
    for slot in range(1, k_ref.shape[0]):
        k_ref[slot] = jnp.zeros(k_ref.shape[1:], F32)
        v_ref[slot] = jnp.zeros(v_ref.shape[1:], F32)
    vt_ref[...] = _dot_nt(wvt_ref[...], xb).astype(BF16)


def _qkv_proj(x, w, kv_bufs, layer, n_layers, tm):
    m, d = x.shape
    tm = min(tm, m)
    wvt = jnp.transpose(w[:, 2 * DA_QK:])
    in_specs = [pl.BlockSpec((tm, d), lambda i: (i, 0)), pl.BlockSpec(w.shape, lambda i: (0, 0)),
                pl.BlockSpec(wvt.shape, lambda i: (0, 0))]
    args = [x, w, wvt]
    aliases = {}
    if kv_bufs is not None:
        in_specs += [pl.BlockSpec(memory_space=pl.ANY)] * 2
        args += list(kv_bufs)
        aliases = {3: 1, 4: 2}
        kv_spec = pl.BlockSpec((1, tm, DA_QK), lambda i: (layer, i, 0))
    else:
        assert layer == 0
        kv_spec = pl.BlockSpec((n_layers, tm, DA_QK), lambda i: (0, i, 0))
    kv_shape = jax.ShapeDtypeStruct((n_layers, m, DA_QK), F32)
    return pl.pallas_call(
        _qkv_kernel,
        grid=(m // tm,),
        in_specs=in_specs,
        out_specs=[pl.BlockSpec((tm, DA_QK), lambda i: (i, 0)), kv_spec, kv_spec,
                   pl.BlockSpec((DA_HEADS * DA_DV, tm), lambda i: (0, i))],
        out_shape=[jax.ShapeDtypeStruct((m, DA_QK), F32), kv_shape, kv_shape,
                   jax.ShapeDtypeStruct((DA_HEADS * DA_DV, m), BF16)],
        input_output_aliases=aliases,
        compiler_params=_cparams("arbitrary"),
    )(*args)


def _pattn_kernel(qi_ref, ki_ref, q_ref, k_ref, vt_ref, slope_ref, lam_ref, g_ref, o_ref,
                  m0, l0, a0, m1, l1, a1, *, tq, lam_init):
    step = pl.program_id(2)
    qi = qi_ref[step]
    ki = ki_ref[step]
    stats = ((m0, l0, a0), (m1, l1, a1))

    @pl.when(ki == 0)
    def _():
        for m, l, a in stats:
            m[...] = jnp.full(m.shape, -jnp.inf, F32)
            l[...] = jnp.zeros(l.shape, F32)
            a[...] = jnp.zeros(a.shape, F32)

    q = q_ref[...] * (DA_DK ** -0.5)
    k = k_ref[0]
    lane = lax.broadcasted_iota(jnp.int32, q.shape, 1)
    kpos = ki * tq + lax.broadcasted_iota(jnp.int32, k.shape, 0)
    lo_mask = (1 << ALIBI_SPLIT_BITS) - 1
    slope = slope_ref[0]
    b_hi = slope * (kpos - jnp.bitwise_and(kpos, lo_mask)).astype(F32)
    b_lo = slope * jnp.bitwise_and(kpos, lo_mask).astype(F32)
    qs, ks = [], []
    for half in range(2):
        in_half = (lane >= DA_DK) if half else (lane < DA_DK)
        spare = 0 if half else DA_DK
        qs.append(jnp.where(in_half, q, jnp.where((lane == spare) | (lane == spare + 1), 1.0, 0.0))
                  .astype(BF16))
        ks.append(jnp.where(in_half, k, jnp.where(lane == spare, b_hi,
                                                  jnp.where(lane == spare + 1, b_lo, 0.0)))
                  .astype(BF16))
    vt = vt_ref[...]

    def update(masked):
        if masked:
            krow = lax.broadcasted_iota(jnp.int32, (tq, tq), 0)
            qcol = lax.broadcasted_iota(jnp.int32, (tq, tq), 1)
            keep = krow <= qcol
        for qh, kh, (m, l, a) in zip(qs, ks, stats):
            st = _dot_nt(kh, qh)
            if masked:
                st = jnp.where(keep, st, NEG_BIG)
            m_new = jnp.maximum(m[...], jnp.max(st, axis=0, keepdims=True))
            alpha = jnp.exp(m[...] - m_new)
            pt = jnp.exp(st - m_new)
            l[...] = alpha * l[...] + jnp.sum(pt, axis=0, keepdims=True)
            a[...] = alpha * a[...] + _dot(vt, pt.astype(BF16))
            m[...] = m_new

    @pl.when(ki < qi)
    def _():
        update(False)

    @pl.when(ki == qi)
    def _():
        update(True)
        ot = a0[...] / l0[...] - lam_ref[...] * (a1[...] / l1[...])
        ot = ot * lax.rsqrt(jnp.mean(ot * ot, axis=0, keepdims=True) + LN_EPS)
        o_ref[...] = jnp.transpose(ot) * (g_ref[...] * (1.0 - lam_init))


def _pattn(q, kbuf, vt, layer, slopes, lam, subln_g, n_seq, t_len, tq, lam_init):
    nq = t_len // tq
    pairs = [(i, j) for i in range(nq) for j in range(i + 1)]
    qi_tab = jnp.array([p[0] for p in pairs], jnp.int32)
    ki_tab = jnp.array([p[1] for p in pairs], jnp.int32)
    q_spec = pl.BlockSpec((tq, DA_DV), lambda b, h, s, qt, kt: (b * nq + qt[s], h))
    grid_spec = pltpu.PrefetchScalarGridSpec(
        num_scalar_prefetch=2,
        grid=(n_seq, DA_HEADS, len(pairs)),
        in_specs=[q_spec,
                  pl.BlockSpec((1, tq, DA_DV), lambda b, h, s, qt, kt: (layer, b * nq + kt[s], h)),
                  pl.BlockSpec((DA_DV, tq), lambda b, h, s, qt, kt: (h, b * nq + kt[s])),
                  pl.BlockSpec((1, 1, 1), lambda b, h, s, qt, kt: (h, 0, 0)),
                  pl.BlockSpec((1, 1), lambda b, h, s, qt, kt: (0, 0)),
                  pl.BlockSpec((1, DA_DV), lambda b, h, s, qt, kt: (0, 0))],
        out_specs=q_spec,
        scratch_shapes=[pltpu.VMEM((1, tq), F32), pltpu.VMEM((1, tq), F32), pltpu.VMEM((DA_DV, tq), F32)] * 2,
    )
    return pl.pallas_call(
        functools.partial(_pattn_kernel, tq=tq, lam_init=lam_init),
        grid_spec=grid_spec,
        out_shape=jax.ShapeDtypeStruct((n_seq * t_len, DA_HEADS * DA_DV), F32),
        compiler_params=_cparams("arbitrary", "arbitrary", "arbitrary"),
    )(qi_tab, ki_tab, q, kbuf, vt, slopes.reshape(DA_HEADS, 1, 1), lam.reshape(1, 1),
      subln_g.reshape(1, DA_DV))


MAX_PAGES_PER_STEP = 8


def _sattn_kernel(pt_ref, q_ref, kn_ref, vn_ref, cmat_ref, slope_ref, lam_ref, g_ref, *rest,
                  n_pages, pps, t_new, lam_init):
    kp_refs = rest[:pps]
    vp_refs = rest[pps:2 * pps]
    o_ref, m_scr, l_scr, a_scr = rest[2 * pps:]
    p = pl.program_id(1)
    nrow = 2 * t_new * DA_HEADS

    @pl.when(p == 0)
    def _():
        m_scr[...] = jnp.full(m_scr.shape, -jnp.inf, F32)
        l_scr[...] = jnp.zeros(l_scr.shape, F32)
        a_scr[...] = jnp.zeros(a_scr.shape, F32)

    q = q_ref[0] * (DA_DK ** -0.5)
    lane = lax.broadcasted_iota(jnp.int32, q.shape, 1)
    qrows = jnp.concatenate([jnp.where(lane < DA_DK, q, 0.0), jnp.where(lane >= DA_DK, q, 0.0)],
                            axis=0).astype(BF16)
    slope = slope_ref[...]

    def accumulate(scores, values):
        m_old = m_scr[...]
        m_new = m_old
        for s in scores:
            m_new = jnp.maximum(m_new, jnp.max(s, axis=1, keepdims=True))
        alpha = jnp.exp(m_old - m_new)
        l_new = alpha * l_scr[...]
        acc = alpha * a_scr[...]
        for s, vb in zip(scores, values):
            pr = jnp.exp(s - m_new)
            l_new = l_new + jnp.sum(pr, axis=1, keepdims=True)
            acc = acc + _dot(pr.astype(BF16), vb)
        m_scr[...] = m_new
        l_scr[...] = l_new
        a_scr[...] = acc

    cmat = cmat_ref[...]
    scores, values = [], []
    for i in range(pps):
        page_start = ((p * pps + i) * PAGE_SIZE).astype(F32)
        s = _dot_nt(qrows, kp_refs[i][0, 0].astype(BF16))
        scores.append(s + (cmat + slope * page_start))
        values.append(vp_refs[i][0, 0].astype(BF16))
    accumulate(scores, values)

    @pl.when(p == n_pages // pps - 1)
    def _():
        ncol = t_new * DA_HEADS
        r = lax.broadcasted_iota(jnp.int32, (nrow, ncol), 0)
        c = lax.broadcasted_iota(jnp.int32, (nrow, ncol), 1)
        hshift = DA_HEADS.bit_length() - 1
        dist = jnp.bitwise_and(jnp.right_shift(r, hshift), t_new - 1) - jnp.right_shift(c, hshift)
        ok = (jnp.bitwise_and(r, DA_HEADS - 1) == jnp.bitwise_and(c, DA_HEADS - 1)) & (dist >= 0)
        s = _dot_nt(qrows, kn_ref[0].astype(BF16))
        s = jnp.where(ok, s - slope * dist.astype(F32), NEG_BIG)
        accumulate([s], [vn_ref[0].astype(BF16)])
        on = a_scr[...] / l_scr[...]
        hr = nrow // 2
        o = on[:hr] - lam_ref[...] * on[hr:]
        o = o * lax.rsqrt(jnp.mean(o * o, axis=-1, keepdims=True) + LN_EPS) * g_ref[...]
        o_ref[0] = o * (1.0 - lam_init)


def _sattn(zq, zk, zv, cache_k, cache_v, page_table, layer, slopes, lam, subln_g, lam_init):
    n_seq, n_pages = page_table.shape
    pps = math.gcd(n_pages, MAX_PAGES_PER_STEP)
    hr = zq.shape[1]
    t_new = hr // DA_HEADS
    nrow = 2 * hr
    past = n_pages * PAGE_SIZE
    slope_col = jnp.tile(slopes, 2 * t_new).reshape(nrow, 1)
    r = jnp.arange(nrow)[:, None]
    c = jnp.arange(PAGE_SIZE * DA_HEADS)[None, :]
    dist0 = (past + (r // DA_HEADS) % t_new - c // DA_HEADS).astype(F32)
    cmat = jnp.where(r % DA_HEADS == c % DA_HEADS, -slope_col * dist0, NEG_BIG).astype(F32)
    seq_spec = pl.BlockSpec((1, hr, DA_DV), lambda s, p, pt: (s, 0, 0))
    const = lambda shape: pl.BlockSpec(shape, lambda s, p, pt: (0,) * len(shape))

    def page_spec(i):
        return pl.BlockSpec((1, 1, PAGE_SIZE * DA_HEADS, DA_DV),
                            lambda s, p, pt: (layer, pt[s * n_pages + p * pps + i], 0, 0))

    grid_spec = pltpu.PrefetchScalarGridSpec(
        num_scalar_prefetch=1,
        grid=(n_seq, n_pages // pps),
        in_specs=[seq_spec, seq_spec, seq_spec, const(cmat.shape), const((nrow, 1)), const((1, 1)),
                  const((1, DA_DV))] + [page_spec(i) for i in range(pps)] * 2,
        out_specs=seq_spec,
        scratch_shapes=[pltpu.VMEM((nrow, 1), F32), pltpu.VMEM((nrow, 1), F32),
                        pltpu.VMEM((nrow, DA_DV), F32)],
    )
    return pl.pallas_call(
        functools.partial(_sattn_kernel, n_pages=n_pages, pps=pps, t_new=t_new, lam_init=lam_init),
        grid_spec=grid_spec,
        out_shape=jax.ShapeDtypeStruct(zq.shape, F32),
        compiler_params=_cparams("arbitrary", "arbitrary"),
    )(page_table.reshape(-1), zq, zk, zv, cmat, slope_col, lam.reshape(1, 1),
      subln_g.reshape(1, DA_DV), *([cache_k] * pps), *([cache_v] * pps))


def _split_bf16(a):
    hi = a.astype(BF16)
    lo = (a - hi.astype(F32)).astype(BF16)
    return hi, lo


def _route(x, w, b):
    xh, xl = _split_bf16(x)
    wh, wl = _split_bf16(w)
    logits = _dot(xh, wh) + (_dot(xl, wh) + _dot(xh, wl)) + b
    lane = lax.broadcasted_iota(jnp.int32, logits.shape, 1).astype(F32)
    big = float(LANES)
    is_grp = lane < MOE_GROUPS
    gl = jnp.where(is_grp, logits, -jnp.inf)
    gmax = jnp.max(gl, axis=1, keepdims=True)
    gidx = jnp.min(jnp.where(gl == gmax, lane, big), axis=1, keepdims=True)
    den = jnp.sum(jnp.where(is_grp, jnp.exp(logits - gmax), 0.0), axis=1, keepdims=True)
    p_g = 1.0 / den
    lo = ROUTE_OFF + MOE_PER_GROUP * gidx
    in_grp = (lane >= lo) & (lane < lo + MOE_PER_GROUP)
    el = jnp.where(in_grp, logits, -jnp.inf)
    v1 = jnp.max(el, axis=1, keepdims=True)
    i1 = jnp.min(jnp.where(el == v1, lane, big), axis=1, keepdims=True)
    el2 = jnp.where(lane == i1, -jnp.inf, el)
    v2 = jnp.max(el2, axis=1, keepdims=True)
    i2 = jnp.min(jnp.where(el2 == v2, lane, big), axis=1, keepdims=True)
    e2 = jnp.exp(v2 - v1)
    w1 = (1.0 / (1.0 + e2)) * p_g
    w2 = (e2 / (1.0 + e2)) * p_g
    gates = jnp.where(lane == i1, w1, 0.0) + jnp.where(lane == i2, w2, 0.0)
    return gates, gidx, i1, i2


def _router_kernel(x_ref, w_ref, b_ref, g_ref):
    g_ref[...] = _route(x_ref[...], w_ref[...], b_ref[...])[0]


def _router(x, w_route, b_route, tm):
    m, d = x.shape
    tm = min(tm, m)
    return pl.pallas_call(
        _router_kernel,
        grid=(m // tm,),
        in_specs=[pl.BlockSpec((tm, d), lambda i: (i, 0)),
                  pl.BlockSpec((d, LANES), lambda i: (0, 0)),
                  pl.BlockSpec((1, LANES), lambda i: (0, 0))],
        out_specs=pl.BlockSpec((tm, LANES), lambda i: (i, 0)),
        out_shape=jax.ShapeDtypeStruct((m, LANES), F32),
        compiler_params=_cparams("arbitrary"),
    )(x, w_route, b_route)


def _moe_kernel(x_ref, g_ref, wg_ref, wu_ref, wd_ref, lg_ref, lb_ref, o_ref, xb_scr, y_scr):
    e = pl.program_id(1)

    @pl.when(e == 0)
    def _():
        xb_scr[...] = x_ref[...].astype(BF16)
        y_scr[...] = jnp.zeros(y_scr.shape, F32)

    lane = lax.broadcasted_iota(jnp.int32, g_ref.shape, 1)
    ge = jnp.sum(jnp.where(lane == e + ROUTE_OFF, g_ref[...], 0.0), axis=1, keepdims=True)
    xb = xb_scr[...]
    h = _silu(_dot(xb, wg_ref[0])) * _dot(xb, wu_ref[0]) * ge
    y_scr[...] += _dot(h.astype(BF16), wd_ref[0])

    @pl.when(e == MOE_EXPERTS - 1)
    def _():
        o_ref[...] = _layer_norm(DN_ALPHA * x_ref[...] + y_scr[...], lg_ref[...], lb_ref[...])


def _moe_ln(x, gates, wg, wu, wd, g, b, tm):
    m, d = x.shape
    tm = min(tm, m)
    return pl.pallas_call(
        _moe_kernel,
        grid=(m // tm, MOE_EXPERTS),
        in_specs=[pl.BlockSpec((tm, d), lambda i, e: (i, 0)),
                  pl.BlockSpec((tm, LANES), lambda i, e: (i, 0)),
                  pl.BlockSpec((1, d, MOE_FF), lambda i, e: (e, 0, 0)),
                  pl.BlockSpec((1, d, MOE_FF), lambda i, e: (e, 0, 0)),
                  pl.BlockSpec((1, MOE_FF, d), lambda i, e: (e, 0, 0)),
                  pl.BlockSpec((1, d), lambda i, e: (0, 0)),
                  pl.BlockSpec((1, d), lambda i, e: (0, 0))],
        out_specs=pl.BlockSpec((tm, d), lambda i, e: (i, 0)),
        out_shape=jax.ShapeDtypeStruct((m, d), F32),
        scratch_shapes=[pltpu.VMEM((tm, d), BF16), pltpu.VMEM((tm, d), F32)],
        compiler_params=_cparams("arbitrary", "arbitrary"),
    )(x, gates, wg, wu, wd, g.reshape(1, d), b.reshape(1, d))


MOE_PAIRS = [(a, b) for a in range(MOE_PER_GROUP) for b in range(a + 1, MOE_PER_GROUP)]
MOE_CLASSES = MOE_GROUPS * len(MOE_PAIRS)
CLASS_ELO = [g * MOE_PER_GROUP + a for g in range(MOE_GROUPS) for a, _ in MOE_PAIRS]
CLASS_EHI = [g * MOE_PER_GROUP + b for g in range(MOE_GROUPS) for _, b in MOE_PAIRS]
MOE_TILE = 256
PERM_CHUNK = 2048


def _router_sort_kernel(x_ref, w_ref, b_ref, cls_ref, rank_ref, cnt_ref, run_scr, *, n_steps):
    i = pl.program_id(0)

    @pl.when(i == 0)
    def _():
        run_scr[...] = jnp.zeros(run_scr.shape, F32)

    _, gidx, i1, i2 = _route(x_ref[...], w_ref[...], b_ref[...])
    base = ROUTE_OFF + MOE_PER_GROUP * gidx
    lo = jnp.minimum(i1, i2) - base
    hi = jnp.maximum(i1, i2) - base
    pair = jnp.where(lo == 0.0, hi - 1.0, jnp.where(lo == 1.0, hi + 1.0, float(len(MOE_PAIRS) - 1)))
    cls = gidx * float(len(MOE_PAIRS)) + pair
    tm = cls.shape[0]
    lane = lax.broadcasted_iota(jnp.int32, (tm, LANES), 1).astype(F32)
    onehot = jnp.where(lane == cls, 1.0, 0.0)
    r = lax.broadcasted_iota(jnp.int32, (tm, tm), 0)
    c = lax.broadcasted_iota(jnp.int32, (tm, tm), 1)
    earlier = jnp.where(c < r, 1.0, 0.0).astype(BF16)
    within = _dot(earlier, onehot.astype(BF16))
    rank = jnp.sum(onehot * (within + run_scr[...]), axis=1, keepdims=True)
    run_scr[...] += jnp.sum(onehot, axis=0, keepdims=True)
    cls_ref[...] = cls.astype(jnp.int32)
    rank_ref[...] = rank.astype(jnp.int32)

    @pl.when(i == n_steps - 1)
    def _():
        cnt_ref[...] = run_scr[...]


def _router_sort(x, w_route, b_route, tm):
    m, d = x.shape
    tm = min(tm, m)
    col = pl.BlockSpec((tm, 1), lambda i: (i, 0))
    return pl.pallas_call(
        functools.partial(_router_sort_kernel, n_steps=m // tm),
        grid=(m // tm,),
        in_specs=[pl.BlockSpec((tm, d), lambda i: (i, 0)),
                  pl.BlockSpec((d, LANES), lambda i: (0, 0)),
                  pl.BlockSpec((1, LANES), lambda i: (0, 0))],
        out_specs=[col, col, pl.BlockSpec((1, LANES), lambda i: (0, 0))],
        out_shape=[jax.ShapeDtypeStruct((m, 1), jnp.int32), jax.ShapeDtypeStruct((m, 1), jnp.int32),
                   jax.ShapeDtypeStruct((1, LANES), F32)],
        scratch_shapes=[pltpu.VMEM((1, LANES), F32)],
        compiler_params=_cparams("arbitrary"),
    )(x, w_route, b_route)


ROW_SLABS = D_MODEL // LANES


def _to_slabs(rows, slab_ref):
    for j in range(ROW_SLABS):
        slab_ref[:, j, :] = rows[:, j * LANES:(j + 1) * LANES]


def _from_slabs(slab_ref):
    return jnp.concatenate([slab_ref[:, j, :] for j in range(ROW_SLABS)], axis=1)


def _run_row_copies(row_copy, chunk):
    def start(r, carry):
        row_copy(r).start()
        return carry

    def wait(r, carry):
        row_copy(r).wait()
        return carry

    lax.fori_loop(0, chunk, start, 0, unroll=8)
    lax.fori_loop(0, chunk, wait, 0, unroll=8)


def _scatter_kernel(idx_ref, x_ref, dst_in_ref, dst_ref, slab_scr, sem, *, chunk):
    del dst_in_ref
    _to_slabs(x_ref[...], slab_scr)
    _run_row_copies(
        lambda r: pltpu.make_async_copy(slab_scr.at[r], dst_ref.at[idx_ref[0, 0, r]], sem), chunk)


def _gather_kernel(idx_ref, src_ref, o_ref, slab_scr, sem, *, chunk):
    _run_row_copies(
        lambda r: pltpu.make_async_copy(src_ref.at[idx_ref[0, 0, r]], slab_scr.at[r], sem), chunk)
    o_ref[...] = _from_slabs(slab_scr)


def _permute_rows(src, idx, dst_init, gather):
    n = idx.shape[0]
    chunk = min(PERM_CHUNK, n)
    any_spec = pl.BlockSpec(memory_space=pl.ANY)
    idx_spec = pl.BlockSpec((1, 1, chunk), lambda i: (i, 0, 0), memory_space=pltpu.SMEM)
    row_spec = pl.BlockSpec((chunk, D_MODEL), lambda i: (i, 0))
    scratch = [pltpu.VMEM((chunk, ROW_SLABS, LANES), F32), pltpu.SemaphoreType.DMA(())]
    params = pltpu.CompilerParams(dimension_semantics=("arbitrary",), vmem_limit_bytes=VMEM_LIMIT,
                                  disable_bounds_checks=True)
    idx3 = idx.reshape(n // chunk, 1, chunk)
    if gather:
        return pl.pallas_call(
            functools.partial(_gather_kernel, chunk=chunk),
            grid=(n // chunk,),
            in_specs=[idx_spec, any_spec],
            out_specs=row_spec,
            out_shape=jax.ShapeDtypeStruct((n, D_MODEL), F32),
            scratch_shapes=scratch,
            compiler_params=params,
        )(idx3, src)
    return pl.pallas_call(
        functools.partial(_scatter_kernel, chunk=chunk),
        grid=(n // chunk,),
        in_specs=[idx_spec, row_spec, any_spec],
        out_specs=any_spec,
        out_shape=jax.ShapeDtypeStruct(dst_init.shape, F32),
        scratch_shapes=scratch,
        input_output_aliases={2: 0},
        compiler_params=params,
    )(idx3, src, dst_init)


def _moe_tile_kernel(elo_ref, ehi_ref, ok_ref, x_ref, wr_ref, br_ref,
                     wg0, wu0, wd0, wg1, wu1, wd1, lg_ref, lb_ref, o_ref):
    t = pl.program_id(0)

    @pl.when(ok_ref[t] == 0)
    def _():
        o_ref[...] = jnp.zeros(o_ref.shape, F32)

    @pl.when(ok_ref[t] != 0)
    def _():
        x = _from_slabs(x_ref)
        gates = _route(x, wr_ref[...], br_ref[...])[0]
        lane = lax.broadcasted_iota(jnp.int32, gates.shape, 1)
        xb = x.astype(BF16)
        y = None
        for e, wg, wu, wd in ((elo_ref[t], wg0, wu0, wd0), (ehi_ref[t], wg1, wu1, wd1)):
            ge = jnp.sum(jnp.where(lane == e + ROUTE_OFF, gates, 0.0), axis=1, keepdims=True)
            h = _silu(_dot(xb, wg[0])) * _dot(xb, wu[0]) * ge
            ye = _dot(h.astype(BF16), wd[0])
            y = ye if y is None else y + ye
        _to_slabs(_layer_norm(DN_ALPHA * x + y, lg_ref[...], lb_ref[...]), o_ref)


def _moe_sorted_ln(x, w_route, b_route, wg, wu, wd, g, b, tm_router):
    n, d = x.shape
    tile = MOE_TILE
    n_tiles = n // tile + MOE_CLASSES
    cls, rank, cnt = _router_sort(x, w_route, b_route, tm_router)
    counts = cnt[0, :MOE_CLASSES].astype(jnp.int32)
    tiles_c = (counts + tile - 1) // tile
    tile_end = jnp.cumsum(tiles_c)
    tile_start = tile_end - tiles_c
    dest = jnp.take(tile_start, cls[:, 0]) * tile + rank[:, 0]
    tids = jnp.arange(n_tiles, dtype=jnp.int32)
    n_used = tile_end[-1]
    t_cls = jnp.searchsorted(tile_end, jnp.minimum(tids, n_used - 1), side='right').astype(jnp.int32)
    t_elo = jnp.take(jnp.array(CLASS_ELO, jnp.int32), t_cls)
    t_ehi = jnp.take(jnp.array(CLASS_EHI, jnp.int32), t_cls)
    t_ok = (tids < n_used).astype(jnp.int32)

    xs = _permute_rows(x, dest, jnp.zeros((n_tiles * tile, ROW_SLABS, LANES), F32), gather=False)
    wspec = lambda tab_pos: pl.BlockSpec(
        (1,) + wg.shape[1:], lambda t, elo, ehi, ok: ((elo, ehi)[tab_pos][t], 0, 0))
    dspec = lambda tab_pos: pl.BlockSpec(
        (1,) + wd.shape[1:], lambda t, elo, ehi, ok: ((elo, ehi)[tab_pos][t], 0, 0))
    const = lambda shape: pl.BlockSpec(shape, lambda t, elo, ehi, ok: (0,) * len(shape))
    row_spec = pl.BlockSpec((tile, ROW_SLABS, LANES), lambda t, elo, ehi, ok: (t, 0, 0))
    grid_spec = pltpu.PrefetchScalarGridSpec(
        num_scalar_prefetch=3,
        grid=(n_tiles,),
        in_specs=[row_spec, const(w_route.shape), const(b_route.shape),
                  wspec(0), wspec(0), dspec(0), wspec(1), wspec(1), dspec(1),
                  const((1, d)), const((1, d))],
        out_specs=row_spec,
    )
    ys = pl.pallas_call(
        _moe_tile_kernel,
        grid_spec=grid_spec,
        out_shape=jax.ShapeDtypeStruct(xs.shape, F32),
        compiler_params=_cparams("arbitrary"),
    )(t_elo, t_ehi, t_ok, xs, w_route, b_route, wg, wu, wd, wg, wu, wd, g.reshape(1, d), b.reshape(1, d))
    return _permute_rows(ys, dest, None, gather=True)


def _pad_time(a, n_seq, t_len, t_pad):
    w = a.shape[1]
    a = a.reshape(n_seq, t_len, w)
    a = jnp.pad(a, ((0, 0), (0, t_pad - t_len), (0, 0)))
    return a.reshape(n_seq * t_pad, w)


def _forward(x_prompt, x_sample, cache_k, cache_v, state_hgrn, state_s5_re, state_s5_im, page_table,
             w_in_ab, w_out_ab, hg_lb_param, hg_norm_g, s5_lam_re, s5_lam_im, s5_log_step,
             s5_b_re, s5_b_im, s5_c_re, s5_c_im, s5_d, s5_w_glu,
             w_in_c, w_out_c, da_lam_q1, da_lam_k1, da_lam_q2, da_lam_k2, da_subln_g,
             moe_w_group, moe_b_group, moe_w_expert, moe_b_expert, moe_w_gate, moe_w_up, moe_w_down,
             ln_g, ln_b):
    bp, tp, d = x_prompt.shape
    bs, ts, _ = x_sample.shape
    ts_pad = SUBLANES
    n_pool = cache_k.shape[1]
    xp = x_prompt.reshape(bp * tp, d)
    xs = x_sample.reshape(bs * ts, d)
    tm_p = min(512, bp * tp)
    tq = min(512, tp)
    s5_tb = min(256, tp)

    slopes = 2.0 ** (-8.0 * jnp.arange(1, DA_HEADS + 1, dtype=F32) / DA_HEADS)
    sm = jax.nn.softmax(hg_lb_param.astype(F32), axis=0)
    lbs = jnp.cumsum(sm, axis=0) - sm[:1]
    ck = cache_k.reshape(cache_k.shape[0], n_pool, PAGE_SIZE * DA_HEADS, 2 * DA_DK)
    cv = cache_v.reshape(cache_v.shape[0], n_pool, PAGE_SIZE * DA_HEADS, DA_DV)
    zeros_hg = jnp.zeros((bp, HG_HEADS, HG_DK, HG_DK), F32)
    zeros_s5 = jnp.zeros((bp, 1, S5_N), F32)

    new_hg_p, new_re_p, new_im_p = [], [], []
    kv_bufs_p = None
    new_k_s, new_v_s, new_hg_s, new_re_s, new_im_s = [], [], [], [], []
    for i in range(DEPTH):
        j = i // 2
        if i % 2 == 0:
            w_in = w_in_ab[j].astype(BF16)
            w_out = w_out_ab[j].astype(BF16)
            s5w = _s5_weights(s5_lam_re[j], s5_lam_im[j], s5_log_step[j], s5_b_re[j], s5_b_im[j],
                              s5_c_re[j], s5_c_im[j], s5_d[j], s5_w_glu[j])
            zp = _matmul(xp, w_in, tm_p)
            oa_p, hg_p = _hgrn(zp, lbs[j], hg_norm_g[j], zeros_hg, bp, tp, math.gcd(tp, HG_CHUNK), tp)
            ob_p, re_p, im_p = _s5(zp, s5w, zeros_s5, zeros_s5, bp, tp, s5_tb, False, SUBLANES)
            xp = _proj_ln([oa_p, ob_p], [w_out[:HG_WIDTH], w_out[HG_WIDTH:]], xp,
                          ln_g[i, 0], ln_b[i, 0], tm_p)
            zs = _pad_time(_matmul(xs, w_in, bs * ts), bs, ts, ts_pad)
            s0t = jnp.swapaxes(state_hgrn[j], -1, -2)
            oa_s, hg_s = _hgrn(zs, lbs[j], hg_norm_g[j], s0t, bs, ts_pad, ts_pad, ts)
            ob_s, re_s, im_s = _s5(zs, s5w, state_s5_re[j].reshape(bs, 1, S5_N),
                                   state_s5_im[j].reshape(bs, 1, S5_N), bs, ts_pad, ts_pad, True, ts)
            unpad = lambda a: a.reshape(bs, ts_pad, -1)[:, :ts].reshape(bs * ts, -1)
            xs = _proj_ln([unpad(oa_s), unpad(ob_s)], [w_out[:HG_WIDTH], w_out[HG_WIDTH:]], xs,
                          ln_g[i, 0], ln_b[i, 0], bs * ts)
            new_hg_p.append(jnp.swapaxes(hg_p, -1, -2))
            new_hg_s.append(jnp.swapaxes(hg_s, -1, -2))
            new_re_p.append(re_p.reshape(bp, S5_GROUPS, S5_STATE))
            new_im_p.append(im_p.reshape(bp, S5_GROUPS, S5_STATE))
            new_re_s.append(re_s.reshape(bs, S5_GROUPS, S5_STATE))
            new_im_s.append(im_s.reshape(bs, S5_GROUPS, S5_STATE))
        else:
            w_in = w_in_c[j].astype(BF16)
            w_out = w_out_c[j].astype(BF16)
            lam_init = 0.8 - 0.6 * math.exp(-0.3 * i)
            lam = (jnp.exp(jnp.sum(da_lam_q1[j].astype(F32) * da_lam_k1[j].astype(F32)))
                   - jnp.exp(jnp.sum(da_lam_q2[j].astype(F32) * da_lam_k2[j].astype(F32))) + lam_init)
            qp, kbuf_p, vbuf_p, vt_p = _qkv_proj(xp, w_in, kv_bufs_p, j, DEPTH // 2, tm_p // 2)
            kv_bufs_p = (kbuf_p, vbuf_p)
            oc_p = _pattn(qp, kbuf_p, vt_p, j, slopes, lam, da_subln_g[j], bp, tp, tq, lam_init)
            xp = _proj_ln([oc_p], [w_out], xp, ln_g[i, 0], ln_b[i, 0], tm_p)

            zs = _matmul(xs, w_in, bs * ts)
            rows = lambda a: a.reshape(bs, ts * DA_HEADS, DA_DV)
            zq_s, zk_s, zv_s = zs[:, :DA_QK], zs[:, DA_QK:2 * DA_QK], zs[:, 2 * DA_QK:]
            oc_s = _sattn(rows(zq_s), rows(zk_s), rows(zv_s), ck, cv, page_table, j,
                          slopes, lam, da_subln_g[j], lam_init)
            new_k_s.append(zk_s.reshape(bs, ts, DA_HEADS, 2 * DA_DK))
            new_v_s.append(zv_s.reshape(bs, ts, DA_HEADS, DA_DV))
            xs = _proj_ln([oc_s.reshape(bs * ts, DA_HEADS * DA_DV)], [w_out], xs,
                          ln_g[i, 0], ln_b[i, 0], bs * ts)

        w_route = jnp.pad(jnp.concatenate([moe_w_group[i], moe_w_expert[i]], axis=1).astype(F32),
                          ((0, 0), (0, LANES - MOE_GROUPS - MOE_EXPERTS)))
        b_route = jnp.pad(jnp.concatenate([moe_b_group[i], moe_b_expert[i]]).astype(F32),
                          (0, LANES - MOE_GROUPS - MOE_EXPERTS)).reshape(1, LANES)
        wg = moe_w_gate[i].astype(BF16)
        wu = moe_w_up[i].astype(BF16)
        wd = moe_w_down[i].astype(BF16)
        xp = _moe_sorted_ln(xp, w_route, b_route, wg, wu, wd, ln_g[i, 1], ln_b[i, 1], tm_p)
        xs = _moe_ln(xs, _router(xs, w_route, b_route, bs * ts), wg, wu, wd,
                     ln_g[i, 1], ln_b[i, 1], bs * ts)

    return (xp.reshape(bp, tp, d), xs.reshape(bs, ts, d),
            kv_bufs_p[0].reshape(DEPTH // 2, bp, tp, DA_HEADS, 2 * DA_DK),
            kv_bufs_p[1].reshape(DEPTH // 2, bp, tp, DA_HEADS, DA_DV), jnp.stack(new_hg_p),
            jnp.stack(new_re_p), jnp.stack(new_im_p),
            jnp.stack(new_k_s), jnp.stack(new_v_s), jnp.stack(new_hg_s),
            jnp.stack(new_re_s), jnp.stack(new_im_s))


_forward_jit = jax.jit(_forward)


def kernel(x_prompt, x_sample, cache_k, cache_v, state_hgrn, state_s5_re, state_s5_im, page_table, w_in_ab, w_out_ab, hg_lb_param, hg_norm_g, s5_lam_re, s5_lam_im, s5_log_step, s5_b_re, s5_b_im, s5_c_re, s5_c_im, s5_d, s5_w_glu, w_in_c, w_out_c, da_lam_q1, da_lam_k1, da_lam_q2, da_lam_k2, da_subln_g, moe_w_group, moe_b_group, moe_w_expert, moe_b_expert, moe_w_gate, moe_w_up, moe_w_down, ln_g, ln_b):
    return _forward_jit(x_prompt, x_sample, cache_k, cache_v, state_hgrn, state_s5_re, state_s5_im, page_table, w_in_ab, w_out_ab, hg_lb_param, hg_norm_g, s5_lam_re, s5_lam_im, s5_log_step, s5_b_re, s5_b_im, s5_c_re, s5_c_im, s5_d, s5_w_glu, w_in_c, w_out_c, da_lam_q1, da_lam_k1, da_lam_q2, da_lam_k2, da_subln_g, moe_w_group, moe_b_group, moe_w_expert, moe_b_expert, moe_w_gate, moe_w_up, moe_w_down, ln_g, ln_b)
```

```python
import functools
import math

import jax
import jax.numpy as jnp
from jax import lax
from jax.experimental import pallas as pl
from jax.experimental.pallas import tpu as pltpu

F32 = jnp.float32
BF16 = jnp.bfloat16

D_MODEL = 1024
DEPTH = 4
HG_HEADS = 4
HG_DK = 128
HG_WIDTH = HG_HEADS * HG_DK
S5_WIDTH = 512
S5_GROUPS = 32
S5_GROUP = 16
S5_STATE = 64
S5_N = S5_GROUPS * S5_STATE
DA_HEADS = 8
DA_DK = 64
DA_DV = 128
DA_QK = DA_HEADS * 2 * DA_DK
MOE_GROUPS = 4
MOE_PER_GROUP = 4
MOE_EXPERTS = 16
MOE_FF = 512
PAGE_SIZE = 128
DN_ALPHA = (2.0 * DEPTH) ** 0.25
LN_EPS = 1e-5
NEG_BIG = -1e30
LB_FLOOR = 1e-30

SUBLANES = 8
LANES = 128
VMEM_LIMIT = 48 * 1024 * 1024

HG_CHUNK = 64
ROUTE_OFF = MOE_GROUPS


def _cparams(*sem):
    return pltpu.CompilerParams(dimension_semantics=sem, vmem_limit_bytes=VMEM_LIMIT)


def _dot(a, b):
    return jnp.dot(a, b, preferred_element_type=F32)


def _dot_nt(a, b):
    return lax.dot_general(a, b, (((1,), (1,)), ((), ())), preferred_element_type=F32)


def _dot_tn(a, b):
    return lax.dot_general(a, b, (((0,), (0,)), ((), ())), preferred_element_type=F32)


def _sigmoid(x):
    return 1.0 / (1.0 + jnp.exp(-x))


def _silu(x):
    return x * _sigmoid(x)


def _expm1(x):
    u = jnp.exp(x)
    um1 = u - 1.0
    r = jnp.where(u == 1.0, x, um1 * x / jnp.log(u))
    return jnp.where(um1 == -1.0, -1.0, r)


def _layer_norm(y, g, b):
    mu = jnp.mean(y, axis=-1, keepdims=True)
    yc = y - mu
    var = jnp.mean(yc * yc, axis=-1, keepdims=True)
    return yc * lax.rsqrt(var + LN_EPS) * g + b


def _mm_kernel(x_ref, w_ref, o_ref):
    o_ref[...] = _dot(x_ref[...].astype(BF16), w_ref[...])


def _matmul(x, w, tm):
    m, k = x.shape
    n = w.shape[1]
    tm = min(tm, m)
    return pl.pallas_call(
        _mm_kernel,
        grid=(m // tm,),
        in_specs=[pl.BlockSpec((tm, k), lambda i: (i, 0)),
                  pl.BlockSpec((k, n), lambda i: (0, 0))],
        out_specs=pl.BlockSpec((tm, n), lambda i: (i, 0)),
        out_shape=jax.ShapeDtypeStruct((m, n), F32),
        compiler_params=_cparams("arbitrary"),
    )(x, w)


def _proj_ln_kernel(*refs, n_lhs):
    a_refs = refs[:n_lhs]
    w_refs = refs[n_lhs:2 * n_lhs]
    x_ref, g_ref, b_ref, o_ref = refs[2 * n_lhs:]
    acc = _dot(a_refs[0][...].astype(BF16), w_refs[0][...])
    for a_ref, w_ref in zip(a_refs[1:], w_refs[1:]):
        acc = acc + _dot(a_ref[...].astype(BF16), w_ref[...])
    o_ref[...] = _layer_norm(DN_ALPHA * x_ref[...] + acc, g_ref[...], b_ref[...])


def _proj_ln(lhs, ws, x, g, b, tm):
    m, d = x.shape
    tm = min(tm, m)
    n_lhs = len(lhs)
    in_specs = [pl.BlockSpec((tm, a.shape[1]), lambda i: (i, 0)) for a in lhs]
    in_specs += [pl.BlockSpec(w.shape, lambda i: (0, 0)) for w in ws]
    in_specs += [pl.BlockSpec((tm, d), lambda i: (i, 0)),
                 pl.BlockSpec((1, d), lambda i: (0, 0)),
                 pl.BlockSpec((1, d), lambda i: (0, 0))]
    return pl.pallas_call(
        functools.partial(_proj_ln_kernel, n_lhs=n_lhs),
        grid=(m // tm,),
        in_specs=in_specs,
        out_specs=pl.BlockSpec((tm, d), lambda i: (i, 0)),
        out_shape=jax.ShapeDtypeStruct((m, d), F32),
        compiler_params=_cparams("arbitrary"),
    )(*lhs, *ws, x, g.reshape(1, d), b.reshape(1, d))


def _cumsum_rows(x, row):
    n = x.shape[0]
    s = 1
    while s < n:
        x = x + jnp.where(row >= s, pltpu.roll(x, s, axis=0), 0.0)
        s *= 2
    return x


def _hgrn_kernel(zq_ref, zf_ref, zi_ref, zg_ref, lb_ref, ng_ref, s0_ref,
                 o_ref, st_ref, st_scr, *, chunk, t_valid, n_chunks):
    c = pl.program_id(1)
    nb = chunk // SUBLANES

    @pl.when(c == 0)
    def _():
        st_scr[...] = s0_ref[0]

    row = lax.broadcasted_iota(jnp.int32, (chunk, HG_DK), 0)
    row8 = lax.broadcasted_iota(jnp.int32, (SUBLANES, HG_DK), 0)
    rowa = lax.broadcasted_iota(jnp.int32, (SUBLANES, chunk), 0)
    cola = lax.broadcasted_iota(jnp.int32, (SUBLANES, chunk), 1)

    for h in range(HG_HEADS):
        sl = slice(h * HG_DK, (h + 1) * HG_DK)
        zf = zf_ref[:, sl]
        zq = zq_ref[:, sl]
        zg = zg_ref[:, sl]
        v = zi_ref[:, sl]
        lb = lb_ref[:, sl]
        log_lb = jnp.log(jnp.maximum(lb, LB_FLOOR))
        log_sig = jnp.minimum(zf, 0.0) - jnp.log1p(jnp.exp(-jnp.abs(zf)))
        t = jnp.log1p(-lb) + log_sig
        logf = jnp.maximum(log_lb, t) + jnp.log1p(jnp.exp(-jnp.abs(log_lb - t)))
        k = -_expm1(logf)
        q = _silu(zq)
        if t_valid < chunk:
            valid = row < t_valid
            logf = jnp.where(valid, logf, 0.0)
            k = jnp.where(valid, k, 0.0)
        b = _cumsum_rows(logf, row)
        b_last = b[chunk - 1:chunk]
        st = st_scr[h]
        vb = v.astype(BF16)

        o = _dot_nt((q * jnp.exp(b)).astype(BF16), st.astype(BF16))
        blocks = []
        for i in range(nb):
            rs = slice(i * SUBLANES, (i + 1) * SUBLANES)
            qi, ki, bi = q[rs], k[rs], b[rs]
            if i > 0:
                r = b[i * SUBLANES - 1:i * SUBLANES]
                qt = qi * jnp.exp(bi - r)
                kt = k * jnp.exp(jnp.minimum(r - b, 0.0))
                att = _dot_nt(qt.astype(BF16), kt.astype(BF16))
                att = jnp.where(cola < i * SUBLANES, att, 0.0)
            else:
                att = jnp.zeros((SUBLANES, chunk), F32)
            for d in range(SUBLANES):
                if d == 0:
                    prod = qi * ki
                else:
                    kd = pltpu.roll(ki, d, axis=0)
                    bd = pltpu.roll(bi, d, axis=0)
                    dec = jnp.exp(jnp.where(row8 >= d, bi - bd, NEG_BIG))
                    prod = qi * kd * dec
                a = jnp.sum(prod, axis=1, keepdims=True)
                att = att + jnp.where(cola == rowa + (i * SUBLANES - d), a, 0.0)
            blocks.append(att)
        att = blocks[0] if nb == 1 else jnp.concatenate(blocks, axis=0)
        o = o + _dot(att.astype(BF16), vb)

        ke = k * jnp.exp(b_last - b)
        st_scr[h] = jnp.exp(b_last) * st + _dot_tn(vb, ke.astype(BF16))

        on = o * lax.rsqrt(jnp.mean(o * o, axis=-1, keepdims=True) + LN_EPS) * ng_ref[:, sl]
        o_ref[:, sl] = on * _silu(zg)

    @pl.when(c == n_chunks - 1)
    def _():
        st_ref[0] = st_scr[...]


def _hgrn(z, lb, norm_g, s0t, n_seq, t_len, chunk, t_valid):
    n_chunks = t_len // chunk
    zspec = lambda cb: pl.BlockSpec((chunk, HG_WIDTH), lambda b, c: (b * n_chunks + c, cb))
    pspec = pl.BlockSpec((1, HG_WIDTH), lambda b, c: (0, 0))
    sspec = pl.BlockSpec((1, HG_HEADS, HG_DK, HG_DK), lambda b, c: (b, 0, 0, 0))
    return pl.pallas_call(
        functools.partial(_hgrn_kernel, chunk=chunk, t_valid=t_valid, n_chunks=n_chunks),
        grid=(n_seq, n_chunks),
        in_specs=[zspec(0), zspec(1), zspec(2), zspec(3), pspec, pspec, sspec],
        out_specs=[pl.BlockSpec((chunk, HG_WIDTH), lambda b, c: (b * n_chunks + c, 0)), sspec],
        out_shape=[jax.ShapeDtypeStruct((n_seq * t_len, HG_WIDTH), F32),
                   jax.ShapeDtypeStruct(s0t.shape, F32)],
        scratch_shapes=[pltpu.VMEM((HG_HEADS, HG_DK, HG_DK), F32)],
        compiler_params=_cparams("arbitrary", "arbitrary"),
    )(z, z, z, z, lb.reshape(1, HG_WIDTH), norm_g.reshape(1, HG_WIDTH), s0t)


S5_LANE_CHUNK = 512


def _s5_kernel(u_ref, bre_ref, bim_ref, cre_ref, cim_ref, coef_ref, pre_ref, pim_ref, mre_ref, mim_ref,
               d_ref, wglu_ref, h0re_ref, h0im_ref,
               o_ref, hre_ref, him_ref, xr_scr, xi_scr, c_scr,
               *, tb, n_tb, per_group, t_valid):
    tstep = pl.program_id(1)
    u = u_ref[...]
    ub = u.astype(BF16)
    n_chunks = S5_N // S5_LANE_CHUNK
    in_w = S5_WIDTH // n_chunks

    if not per_group:
        @pl.when(tstep == 0)
        def _():
            c_scr[0:1] = h0re_ref[0]
            c_scr[1:2] = h0im_ref[0]

    y_parts = []
    for lc in range(n_chunks):
        ls = slice(lc * S5_LANE_CHUNK, (lc + 1) * S5_LANE_CHUNK)
        ubj = ub[:, lc * in_w:(lc + 1) * in_w]
        xr_raw = _dot(ubj, bre_ref[lc])
        xi_raw = _dot(ubj, bim_ref[lc])
        cfr = coef_ref[0:1, ls]
        cfi = coef_ref[1:2, ls]
        xr_scr[:, ls] = cfr * xr_raw - cfi * xi_raw
        xi_scr[:, ls] = cfr * xi_raw + cfi * xr_raw
        pr = pre_ref[:, ls]
        pi = pim_ref[:, ls]

        def body(g, carry, ls=ls, pr=pr, pi=pi):
            if per_group:
                cr = h0re_ref[g][:, ls]
                ci = h0im_ref[g][:, ls]
            else:
                cr, ci = carry
            rs = pl.ds(pl.multiple_of(g * SUBLANES, SUBLANES), SUBLANES)
            xr = xr_scr[rs, ls]
            xi = xi_scr[rs, ls]
            for k in range(3):
                ar = mre_ref[k * SUBLANES:(k + 1) * SUBLANES, ls]
                ai = mim_ref[k * SUBLANES:(k + 1) * SUBLANES, ls]
                sr = pltpu.roll(xr, 1 << k, axis=0)
                si = pltpu.roll(xi, 1 << k, axis=0)
                xr, xi = xr + (ar * sr - ai * si), xi + (ar * si + ai * sr)
            hr = xr + (pr * cr - pi * ci)
            hi = xi + (pr * ci + pi * cr)
            xr_scr[rs, ls] = hr
            xi_scr[rs, ls] = hi
            if per_group:
                hre_ref[g, :, ls] = hr[t_valid - 1:t_valid]
                him_ref[g, :, ls] = hi[t_valid - 1:t_valid]
            return hr[SUBLANES - 1:SUBLANES], hi[SUBLANES - 1:SUBLANES]

        carry0 = (c_scr[0:1, ls], c_scr[1:2, ls])
        cr, ci = lax.fori_loop(0, tb // SUBLANES, body, carry0)
        if not per_group:
            c_scr[0:1, ls] = cr
            c_scr[1:2, ls] = ci
        y_parts.append(_dot(xr_scr[:, ls].astype(BF16), cre_ref[lc])
                       - _dot(xi_scr[:, ls].astype(BF16), cim_ref[lc]))

    y = jnp.concatenate(y_parts, axis=1) + d_ref[...] * u
    y = y * (0.5 * (1.0 + jnp.tanh(math.sqrt(2.0 / math.pi) * (y + 0.044715 * (y * y * y)))))
    gl = _dot(y.astype(BF16), wglu_ref[...])
    o_ref[...] = gl[:, :S5_WIDTH] * _sigmoid(gl[:, S5_WIDTH:])

    if not per_group:
        @pl.when(tstep == n_tb - 1)
        def _():
            hre_ref[0] = c_scr[0:1]
            him_ref[0] = c_scr[1:2]


def _s5(z, wts, h0re, h0im, n_seq, t_len, tb, per_group, t_valid):
    bre, bim, cre, cim, coef, pre, pim, mre, mim, dvec, wglu = wts
    if per_group:
        grid = (1, 1)
        n_tb = 1
        tb = z.shape[0]
        hspec = pl.BlockSpec(h0re.shape, lambda b, t: (0, 0, 0))
    else:
        n_tb = t_len // tb
        grid = (n_seq, n_tb)
        hspec = pl.BlockSpec((1, 1, S5_N), lambda b, t: (b, 0, 0))
    ucol = (4 * HG_WIDTH) // S5_WIDTH
    full = lambda a: pl.BlockSpec(a.shape, lambda b, t: (0,) * a.ndim)
    return pl.pallas_call(
        functools.partial(_s5_kernel, tb=tb, n_tb=n_tb, per_group=per_group, t_valid=t_valid),
        grid=grid,
        in_specs=[pl.BlockSpec((tb, S5_WIDTH), lambda b, t: (b * n_tb + t, ucol)),
                  full(bre), full(bim), full(cre), full(cim), full(coef), full(pre), full(pim),
                  full(mre), full(mim), full(dvec), full(wglu), hspec, hspec],
        out_specs=[pl.BlockSpec((tb, S5_WIDTH), lambda b, t: (b * n_tb + t, 0)), hspec, hspec],
        out_shape=[jax.ShapeDtypeStruct((z.shape[0], S5_WIDTH), F32),
                   jax.ShapeDtypeStruct(h0re.shape, F32),
                   jax.ShapeDtypeStruct(h0im.shape, F32)],
        scratch_shapes=[pltpu.VMEM((tb, S5_N), F32), pltpu.VMEM((tb, S5_N), F32),
                        pltpu.VMEM((2, S5_N), F32)],
        compiler_params=_cparams("arbitrary", "arbitrary"),
    )(z, bre, bim, cre, cim, coef, pre, pim, mre, mim, dvec, wglu, h0re, h0im)


def _s5_weights(lam_re, lam_im, log_step, b_re, b_im, c_re, c_im, d, w_glu):
    lr, li = lam_re.astype(F32), lam_im.astype(F32)
    dt = jnp.exp(log_step.astype(F32))[:, None]
    mag = jnp.exp(lr * dt)
    ab_re, ab_im = mag * jnp.cos(li * dt), mag * jnp.sin(li * dt)
    den = lr * lr + li * li
    nr, ni = ab_re - 1.0, ab_im
    coef_re = (nr * lr + ni * li) / den
    coef_im = (ni * lr - nr * li) / den
    ar, ai = ab_re.reshape(1, S5_N), ab_im.reshape(1, S5_N)
    pr, pi = [ar], [ai]
    for _ in range(SUBLANES - 1):
        pr, pi = pr + [pr[-1] * ar - pi[-1] * ai], pi + [pr[-1] * ai + pi[-1] * ar]
    pre = jnp.concatenate(pr, axis=0)
    pim = jnp.concatenate(pi, axis=0)
    coef = jnp.concatenate([coef_re.reshape(1, S5_N), coef_im.reshape(1, S5_N)], axis=0)
    rows = jnp.arange(SUBLANES)[:, None]
    masked = lambda p: jnp.concatenate(
        [jnp.where(rows >= (1 << k), p[(1 << k) - 1:(1 << k)], 0.0) for k in range(3)], axis=0)
    n_chunks = S5_N // S5_LANE_CHUNK
    gpc = S5_GROUPS // n_chunks
    eye = jnp.eye(gpc, dtype=F32)
    bd = lambda w: jnp.einsum('jgpc,gh->jgchp', w.reshape(n_chunks, gpc, S5_STATE, S5_GROUP), eye).reshape(
        n_chunks, gpc * S5_GROUP, S5_LANE_CHUNK).astype(BF16)
    cd = lambda w: jnp.einsum('jgcp,gh->jgphc', w.reshape(n_chunks, gpc, S5_GROUP, S5_STATE), eye).reshape(
        n_chunks, S5_LANE_CHUNK, gpc * S5_GROUP).astype(BF16)
    return (bd(b_re), bd(b_im), cd(c_re), cd(c_im), coef, pre, pim, masked(pre), masked(pim),
            d.reshape(1, S5_WIDTH).astype(F32), w_glu.astype(BF16))


PATTN_STRIP = 256
ALIBI_SPLIT_BITS = 7


def _qkv_kernel(*refs):
    x_ref, w_ref, wvt_ref = refs[:3]
    q_ref, k_ref, v_ref, vt_ref = refs[-4:]
    xb = x_ref[...].astype(BF16)
    z = _dot(xb, w_ref[...])
    q_ref[...] = z[:, :DA_QK]
    k_ref[0] = z[:, DA_QK:2 * DA_QK]
    v_ref[0] = z[:, 2 * DA_QK:]
    for slot in range(1, k_ref.shape[0]):
        k_ref[slot] = jnp.zeros(k_ref.shape[1:], F32)
        v_ref[slot] = jnp.zeros(v_ref.shape[1:], F32)
    vt_ref[...] = _dot_nt(wvt_ref[...], xb).astype(BF16)


def _qkv_proj(x, w, kv_bufs, layer, n_layers, tm):
    m, d = x.shape
    tm = min(tm, m)
    wvt = jnp.transpose(w[:, 2 * DA_QK:])
    in_specs = [pl.BlockSpec((tm, d), lambda i: (i, 0)), pl.BlockSpec(w.shape, lambda i: (0, 0)),
                pl.BlockSpec(wvt.shape, lambda i: (0, 0))]
    args = [x, w, wvt]
    aliases = {}
    if kv_bufs is not None:
        in_specs += [pl.BlockSpec(memory_space=pl.ANY)] * 2
        args += list(kv_bufs)
        aliases = {3: 1, 4: 2}
        kv_spec = pl.BlockSpec((1, tm, DA_QK), lambda i: (layer, i, 0))
    else:
        assert layer == 0
        kv_spec = pl.BlockSpec((n_layers, tm, DA_QK), lambda i: (0, i, 0))
    kv_shape = jax.ShapeDtypeStruct((n_layers, m, DA_QK), F32)
    return pl.pallas_call(
        _qkv_kernel,
        grid=(m // tm,),
        in_specs=in_specs,
        out_specs=[pl.BlockSpec((tm, DA_QK), lambda i: (i, 0)), kv_spec, kv_spec,
                   pl.BlockSpec((DA_HEADS * DA_DV, tm), lambda i: (0, i))],
        out_shape=[jax.ShapeDtypeStruct((m, DA_QK), F32), kv_shape, kv_shape,
                   jax.ShapeDtypeStruct((DA_HEADS * DA_DV, m), BF16)],
        input_output_aliases=aliases,
        compiler_params=_cparams("arbitrary"),
    )(*args)


def _pattn_kernel(qi_ref, ki_ref, q_ref, k_ref, vt_ref, slope_ref, lam_ref, g_ref, o_ref,
                  m0, l0, a0, m1, l1, a1, *, tq, lam_init):
    step = pl.program_id(2)
    qi = qi_ref[step]
    ki = ki_ref[step]
    stats = ((m0, l0, a0), (m1, l1, a1))

    @pl.when(ki == 0)
    def _():
        for m, l, a in stats:
            m[...] = jnp.full(m.shape, -jnp.inf, F32)
            l[...] = jnp.zeros(l.shape, F32)
            a[...] = jnp.zeros(a.shape, F32)

    q = q_ref[...] * (DA_DK ** -0.5)
    k = k_ref[0]
    lane = lax.broadcasted_iota(jnp.int32, q.shape, 1)
    kpos = ki * tq + lax.broadcasted_iota(jnp.int32, k.shape, 0)
    lo_mask = (1 << ALIBI_SPLIT_BITS) - 1
    slope = slope_ref[0]
    b_hi = slope * (kpos - jnp.bitwise_and(kpos, lo_mask)).astype(F32)
    b_lo = slope * jnp.bitwise_and(kpos, lo_mask).astype(F32)
    qs, ks = [], []
    for half in range(2):
        in_half = (lane >= DA_DK) if half else (lane < DA_DK)
        spare = 0 if half else DA_DK
        qs.append(jnp.where(in_half, q, jnp.where((lane == spare) | (lane == spare + 1), 1.0, 0.0))
                  .astype(BF16))
        ks.append(jnp.where(in_half, k, jnp.where(lane == spare, b_hi,
                                                  jnp.where(lane == spare + 1, b_lo, 0.0)))
                  .astype(BF16))
    vt = vt_ref[...]

    def update(masked):
        work = []
        strip = min(PATTN_STRIP, tq)
        for c0 in range(0, tq, strip):
            cs = slice(c0, c0 + strip)
            for qh, kh, stat in zip(qs, ks, stats):
                work.append((c0, cs, stat, _dot_nt(kh, qh[cs])))
        probs = []
        for c0, cs, (m, l, a), st in work:
            if masked:
                krow = lax.broadcasted_iota(jnp.int32, (tq, strip), 0)
                qcol = lax.broadcasted_iota(jnp.int32, (tq, strip), 1) + c0
                st = jnp.where(krow <= qcol, st, NEG_BIG)
            m_old = m[:, cs]
            m_new = jnp.maximum(m_old, jnp.max(st, axis=0, keepdims=True))
            alpha = jnp.exp(m_old - m_new)
            pt = jnp.exp(st - m_new)
            l[:, cs] = alpha * l[:, cs] + jnp.sum(pt, axis=0, keepdims=True)
            m[:, cs] = m_new
            probs.append((cs, a, alpha, pt.astype(BF16)))
        for cs, a, alpha, pb in probs:
            a[:, cs] = alpha * a[:, cs] + _dot(vt, pb)

    @pl.when(ki < qi)
    def _():
        update(False)

    @pl.when(ki == qi)
    def _():
        update(True)
        ot = a0[...] / l0[...] - lam_ref[...] * (a1[...] / l1[...])
        ot = ot * lax.rsqrt(jnp.mean(ot * ot, axis=0, keepdims=True) + LN_EPS)
        o_ref[...] = jnp.transpose(ot) * (g_ref[...] * (1.0 - lam_init))


def _pattn(q, kbuf, vt, layer, slopes, lam, subln_g, n_seq, t_len, tq, lam_init):
    nq = t_len // tq
    pairs = [(i, j) for i in range(nq) for j in range(i + 1)]
    qi_tab = jnp.array([p[0] for p in pairs], jnp.int32)
    ki_tab = jnp.array([p[1] for p in pairs], jnp.int32)
    q_spec = pl.BlockSpec((tq, DA_DV), lambda b, h, s, qt, kt: (b * nq + qt[s], h))
    grid_spec = pltpu.PrefetchScalarGridSpec(
        num_scalar_prefetch=2,
        grid=(n_seq, DA_HEADS, len(pairs)),
        in_specs=[q_spec,
                  pl.BlockSpec((1, tq, DA_DV), lambda b, h, s, qt, kt: (layer, b * nq + kt[s], h)),
                  pl.BlockSpec((DA_DV, tq), lambda b, h, s, qt, kt: (h, b * nq + kt[s])),
                  pl.BlockSpec((1, 1, 1), lambda b, h, s, qt, kt: (h, 0, 0)),
                  pl.BlockSpec((1, 1), lambda b, h, s, qt, kt: (0, 0)),
                  pl.BlockSpec((1, DA_DV), lambda b, h, s, qt, kt: (0, 0))],
        out_specs=q_spec,
        scratch_shapes=[pltpu.VMEM((1, tq), F32), pltpu.VMEM((1, tq), F32), pltpu.VMEM((DA_DV, tq), F32)] * 2,
    )
    return pl.pallas_call(
        functools.partial(_pattn_kernel, tq=tq, lam_init=lam_init),
        grid_spec=grid_spec,
        out_shape=jax.ShapeDtypeStruct((n_seq * t_len, DA_HEADS * DA_DV), F32),
        compiler_params=_cparams("arbitrary", "arbitrary", "arbitrary"),
    )(qi_tab, ki_tab, q, kbuf, vt, slopes.reshape(DA_HEADS, 1, 1), lam.reshape(1, 1),
      subln_g.reshape(1, DA_DV))


MAX_PAGES_PER_STEP = 8


def _sattn_kernel(pt_ref, q_ref, kn_ref, vn_ref, cmat_ref, slope_ref, lam_ref, g_ref, *rest,
                  n_pages, pps, t_new, lam_init):
    kp_refs = rest[:pps]
    vp_refs = rest[pps:2 * pps]
    o_ref, m_scr, l_scr, a_scr = rest[2 * pps:]
    p = pl.program_id(1)
    nrow = 2 * t_new * DA_HEADS

    @pl.when(p == 0)
    def _():
        m_scr[...] = jnp.full(m_scr.shape, -jnp.inf, F32)
        l_scr[...] = jnp.zeros(l_scr.shape, F32)
        a_scr[...] = jnp.zeros(a_scr.shape, F32)

    q = q_ref[0] * (DA_DK ** -0.5)
    lane = lax.broadcasted_iota(jnp.int32, q.shape, 1)
    qrows = jnp.concatenate([jnp.where(lane < DA_DK, q, 0.0), jnp.where(lane >= DA_DK, q, 0.0)],
                            axis=0).astype(BF16)
    slope = slope_ref[...]

    def accumulate(scores, values):
        m_old = m_scr[...]
        m_new = m_old
        for s in scores:
            m_new = jnp.maximum(m_new, jnp.max(s, axis=1, keepdims=True))
        alpha = jnp.exp(m_old - m_new)
        l_new = alpha * l_scr[...]
        acc = alpha * a_scr[...]
        for s, vb in zip(scores, values):
            pr = jnp.exp(s - m_new)
            l_new = l_new + jnp.sum(pr, axis=1, keepdims=True)
            acc = acc + _dot(pr.astype(BF16), vb)
        m_scr[...] = m_new
        l_scr[...] = l_new
        a_scr[...] = acc

    cmat = cmat_ref[...]
    scores, values = [], []
    for i in range(pps):
        page_start = ((p * pps + i) * PAGE_SIZE).astype(F32)
        s = _dot_nt(qrows, kp_refs[i][0, 0].astype(BF16))
        scores.append(s + (cmat + slope * page_start))
        values.append(vp_refs[i][0, 0].astype(BF16))
    accumulate(scores, values)

    @pl.when(p == n_pages // pps - 1)
    def _():
        ncol = t_new * DA_HEADS
        r = lax.broadcasted_iota(jnp.int32, (nrow, ncol), 0)
        c = lax.broadcasted_iota(jnp.int32, (nrow, ncol), 1)
        hshift = DA_HEADS.bit_length() - 1
        dist = jnp.bitwise_and(jnp.right_shift(r, hshift), t_new - 1) - jnp.right_shift(c, hshift)
        ok = (jnp.bitwise_and(r, DA_HEADS - 1) == jnp.bitwise_and(c, DA_HEADS - 1)) & (dist >= 0)
        s = _dot_nt(qrows, kn_ref[0].astype(BF16))
        s = jnp.where(ok, s - slope * dist.astype(F32), NEG_BIG)
        accumulate([s], [vn_ref[0].astype(BF16)])
        on = a_scr[...] / l_scr[...]
        hr = nrow // 2
        o = on[:hr] - lam_ref[...] * on[hr:]
        o = o * lax.rsqrt(jnp.mean(o * o, axis=-1, keepdims=True) + LN_EPS) * g_ref[...]
        o_ref[0] = o * (1.0 - lam_init)


def _sattn(zq, zk, zv, cache_k, cache_v, page_table, layer, slopes, lam, subln_g, lam_init):
    n_seq, n_pages = page_table.shape
    pps = math.gcd(n_pages, MAX_PAGES_PER_STEP)
    hr = zq.shape[1]
    t_new = hr // DA_HEADS
    nrow = 2 * hr
    past = n_pages * PAGE_SIZE
    slope_col = jnp.tile(slopes, 2 * t_new).reshape(nrow, 1)
    r = jnp.arange(nrow)[:, None]
    c = jnp.arange(PAGE_SIZE * DA_HEADS)[None, :]
    dist0 = (past + (r // DA_HEADS) % t_new - c // DA_HEADS).astype(F32)
    cmat = jnp.where(r % DA_HEADS == c % DA_HEADS, -slope_col * dist0, NEG_BIG).astype(F32)
    seq_spec = pl.BlockSpec((1, hr, DA_DV), lambda s, p, pt: (s, 0, 0))
    const = lambda shape: pl.BlockSpec(shape, lambda s, p, pt: (0,) * len(shape))

    def page_spec(i):
        return pl.BlockSpec((1, 1, PAGE_SIZE * DA_HEADS, DA_DV),
                            lambda s, p, pt: (layer, pt[s * n_pages + p * pps + i], 0, 0))

    grid_spec = pltpu.PrefetchScalarGridSpec(
        num_scalar_prefetch=1,
        grid=(n_seq, n_pages // pps),
        in_specs=[seq_spec, seq_spec, seq_spec, const(cmat.shape), const((nrow, 1)), const((1, 1)),
                  const((1, DA_DV))] + [page_spec(i) for i in range(pps)] * 2,
        out_specs=seq_spec,
        scratch_shapes=[pltpu.VMEM((nrow, 1), F32), pltpu.VMEM((nrow, 1), F32),
                        pltpu.VMEM((nrow, DA_DV), F32)],
    )
    return pl.pallas_call(
        functools.partial(_sattn_kernel, n_pages=n_pages, pps=pps, t_new=t_new, lam_init=lam_init),
        grid_spec=grid_spec,
        out_shape=jax.ShapeDtypeStruct(zq.shape, F32),
        compiler_params=_cparams("arbitrary", "arbitrary"),
    )(page_table.reshape(-1), zq, zk, zv, cmat, slope_col, lam.reshape(1, 1),
      subln_g.reshape(1, DA_DV), *([cache_k] * pps), *([cache_v] * pps))


def _split_bf16(a):
    hi = a.astype(BF16)
    lo = (a - hi.astype(F32)).astype(BF16)
    return hi, lo


def _route(x, w, b):
    xh, xl = _split_bf16(x)
    wh, wl = _split_bf16(w)
    logits = _dot(xh, wh) + (_dot(xl, wh) + _dot(xh, wl)) + b
    lane = lax.broadcasted_iota(jnp.int32, logits.shape, 1).astype(F32)
    big = float(LANES)
    is_grp = lane < MOE_GROUPS
    gl = jnp.where(is_grp, logits, -jnp.inf)
    gmax = jnp.max(gl, axis=1, keepdims=True)
    gidx = jnp.min(jnp.where(gl == gmax, lane, big), axis=1, keepdims=True)
    den = jnp.sum(jnp.where(is_grp, jnp.exp(logits - gmax), 0.0), axis=1, keepdims=True)
    p_g = 1.0 / den
    lo = ROUTE_OFF + MOE_PER_GROUP * gidx
    in_grp = (lane >= lo) & (lane < lo + MOE_PER_GROUP)
    el = jnp.where(in_grp, logits, -jnp.inf)
    v1 = jnp.max(el, axis=1, keepdims=True)
    i1 = jnp.min(jnp.where(el == v1, lane, big), axis=1, keepdims=True)
    el2 = jnp.where(lane == i1, -jnp.inf, el)
    v2 = jnp.max(el2, axis=1, keepdims=True)
    i2 = jnp.min(jnp.where(el2 == v2, lane, big), axis=1, keepdims=True)
    e2 = jnp.exp(v2 - v1)
    w1 = (1.0 / (1.0 + e2)) * p_g
    w2 = (e2 / (1.0 + e2)) * p_g
    gates = jnp.where(lane == i1, w1, 0.0) + jnp.where(lane == i2, w2, 0.0)
    return gates, gidx, i1, i2


def _router_kernel(x_ref, w_ref, b_ref, g_ref):
    g_ref[...] = _route(x_ref[...], w_ref[...], b_ref[...])[0]


def _router(x, w_route, b_route, tm):
    m, d = x.shape
    tm = min(tm, m)
    return pl.pallas_call(
        _router_kernel,
        grid=(m // tm,),
        in_specs=[pl.BlockSpec((tm, d), lambda i: (i, 0)),
                  pl.BlockSpec((d, LANES), lambda i: (0, 0)),
                  pl.BlockSpec((1, LANES), lambda i: (0, 0))],
        out_specs=pl.BlockSpec((tm, LANES), lambda i: (i, 0)),
        out_shape=jax.ShapeDtypeStruct((m, LANES), F32),
        compiler_params=_cparams("arbitrary"),
    )(x, w_route, b_route)


def _moe_kernel(x_ref, g_ref, wg_ref, wu_ref, wd_ref, lg_ref, lb_ref, o_ref, xb_scr, y_scr):
    e = pl.program_id(1)

    @pl.when(e == 0)
    def _():
        xb_scr[...] = x_ref[...].astype(BF16)
        y_scr[...] = jnp.zeros(y_scr.shape, F32)

    lane = lax.broadcasted_iota(jnp.int32, g_ref.shape, 1)
    ge = jnp.sum(jnp.where(lane == e + ROUTE_OFF, g_ref[...], 0.0), axis=1, keepdims=True)
    xb = xb_scr[...]
    h = _silu(_dot(xb, wg_ref[0])) * _dot(xb, wu_ref[0]) * ge
    y_scr[...] += _dot(h.astype(BF16), wd_ref[0])

    @pl.when(e == MOE_EXPERTS - 1)
    def _():
        o_ref[...] = _layer_norm(DN_ALPHA * x_ref[...] + y_scr[...], lg_ref[...], lb_ref[...])


def _moe_ln(x, gates, wg, wu, wd, g, b, tm):
    m, d = x.shape
    tm = min(tm, m)
    return pl.pallas_call(
        _moe_kernel,
        grid=(m // tm, MOE_EXPERTS),
        in_specs=[pl.BlockSpec((tm, d), lambda i, e: (i, 0)),
                  pl.BlockSpec((tm, LANES), lambda i, e: (i, 0)),
                  pl.BlockSpec((1, d, MOE_FF), lambda i, e: (e, 0, 0)),
                  pl.BlockSpec((1, d, MOE_FF), lambda i, e: (e, 0, 0)),
                  pl.BlockSpec((1, MOE_FF, d), lambda i, e: (e, 0, 0)),
                  pl.BlockSpec((1, d), lambda i, e: (0, 0)),
                  pl.BlockSpec((1, d), lambda i, e: (0, 0))],
        out_specs=pl.BlockSpec((tm, d), lambda i, e: (i, 0)),
        out_shape=jax.ShapeDtypeStruct((m, d), F32),
        scratch_shapes=[pltpu.VMEM((tm, d), BF16), pltpu.VMEM((tm, d), F32)],
        compiler_params=_cparams("arbitrary", "arbitrary"),
    )(x, gates, wg, wu, wd, g.reshape(1, d), b.reshape(1, d))


MOE_PAIRS = [(a, b) for a in range(MOE_PER_GROUP) for b in range(a + 1, MOE_PER_GROUP)]
MOE_CLASSES = MOE_GROUPS * len(MOE_PAIRS)
CLASS_ELO = [g * MOE_PER_GROUP + a for g in range(MOE_GROUPS) for a, _ in MOE_PAIRS]
CLASS_EHI = [g * MOE_PER_GROUP + b for g in range(MOE_GROUPS) for _, b in MOE_PAIRS]
MOE_TILE = 256
PERM_CHUNK = 2048


def _router_sort_kernel(x_ref, w_ref, b_ref, cls_ref, rank_ref, cnt_ref, run_scr, *, n_steps):
    i = pl.program_id(0)

    @pl.when(i == 0)
    def _():
        run_scr[...] = jnp.zeros(run_scr.shape, F32)

    _, gidx, i1, i2 = _route(x_ref[...], w_ref[...], b_ref[...])
    base = ROUTE_OFF + MOE_PER_GROUP * gidx
    lo = jnp.minimum(i1, i2) - base
    hi = jnp.maximum(i1, i2) - base
    pair = jnp.where(lo == 0.0, hi - 1.0, jnp.where(lo == 1.0, hi + 1.0, float(len(MOE_PAIRS) - 1)))
    cls = gidx * float(len(MOE_PAIRS)) + pair
    tm = cls.shape[0]
    lane = lax.broadcasted_iota(jnp.int32, (tm, LANES), 1).astype(F32)
    onehot = jnp.where(lane == cls, 1.0, 0.0)
    r = lax.broadcasted_iota(jnp.int32, (tm, tm), 0)
    c = lax.broadcasted_iota(jnp.int32, (tm, tm), 1)
    earlier = jnp.where(c < r, 1.0, 0.0).astype(BF16)
    within = _dot(earlier, onehot.astype(BF16))
    rank = jnp.sum(onehot * (within + run_scr[...]), axis=1, keepdims=True)
    run_scr[...] += jnp.sum(onehot, axis=0, keepdims=True)
    cls_ref[...] = cls.astype(jnp.int32)
    rank_ref[...] = rank.astype(jnp.int32)

    @pl.when(i == n_steps - 1)
    def _():
        cnt_ref[...] = run_scr[...]


def _router_sort(x, w_route, b_route, tm):
    m, d = x.shape
    tm = min(tm, m)
    col = pl.BlockSpec((tm, 1), lambda i: (i, 0))
    return pl.pallas_call(
        functools.partial(_router_sort_kernel, n_steps=m // tm),
        grid=(m // tm,),
        in_specs=[pl.BlockSpec((tm, d), lambda i: (i, 0)),
                  pl.BlockSpec((d, LANES), lambda i: (0, 0)),
                  pl.BlockSpec((1, LANES), lambda i: (0, 0))],
        out_specs=[col, col, pl.BlockSpec((1, LANES), lambda i: (0, 0))],
        out_shape=[jax.ShapeDtypeStruct((m, 1), jnp.int32), jax.ShapeDtypeStruct((m, 1), jnp.int32),
                   jax.ShapeDtypeStruct((1, LANES), F32)],
        scratch_shapes=[pltpu.VMEM((1, LANES), F32)],
        compiler_params=_cparams("arbitrary"),
    )(x, w_route, b_route)


ROW_SLABS = D_MODEL // LANES


def _to_slabs(rows, slab_ref):
    for j in range(ROW_SLABS):
        slab_ref[:, j, :] = rows[:, j * LANES:(j + 1) * LANES]


def _from_slabs(slab_ref):
    return jnp.concatenate([slab_ref[:, j, :] for j in range(ROW_SLABS)], axis=1)


def _run_row_copies(row_copy, chunk):
    def start(r, carry):
        row_copy(r).start()
        return carry

    def wait(r, carry):
        row_copy(r).wait()
        return carry

    lax.fori_loop(0, chunk, start, 0, unroll=8)
    lax.fori_loop(0, chunk, wait, 0, unroll=8)


def _scatter_kernel(idx_ref, x_ref, dst_in_ref, dst_ref, slab_scr, sem, *, chunk):
    del dst_in_ref
    _to_slabs(x_ref[...], slab_scr)
    _run_row_copies(
        lambda r: pltpu.make_async_copy(slab_scr.at[r], dst_ref.at[idx_ref[0, 0, r]], sem), chunk)


def _gather_kernel(idx_ref, src_ref, o_ref, slab_scr, sem, *, chunk):
    _run_row_copies(
        lambda r: pltpu.make_async_copy(src_ref.at[idx_ref[0, 0, r]], slab_scr.at[r], sem), chunk)
    o_ref[...] = _from_slabs(slab_scr)


def _permute_rows(src, idx, dst_init, gather):
    n = idx.shape[0]
    chunk = min(PERM_CHUNK, n)
    any_spec = pl.BlockSpec(memory_space=pl.ANY)
    idx_spec = pl.BlockSpec((1, 1, chunk), lambda i: (i, 0, 0), memory_space=pltpu.SMEM)
    row_spec = pl.BlockSpec((chunk, D_MODEL), lambda i: (i, 0))
    scratch = [pltpu.VMEM((chunk, ROW_SLABS, LANES), F32), pltpu.SemaphoreType.DMA(())]
    params = pltpu.CompilerParams(dimension_semantics=("arbitrary",), vmem_limit_bytes=VMEM_LIMIT,
                                  disable_bounds_checks=True)
    idx3 = idx.reshape(n // chunk, 1, chunk)
    if gather:
        return pl.pallas_call(
            functools.partial(_gather_kernel, chunk=chunk),
            grid=(n // chunk,),
            in_specs=[idx_spec, any_spec],
            out_specs=row_spec,
            out_shape=jax.ShapeDtypeStruct((n, D_MODEL), F32),
            scratch_shapes=scratch,
            compiler_params=params,
        )(idx3, src)
    return pl.pallas_call(
        functools.partial(_scatter_kernel, chunk=chunk),
        grid=(n // chunk,),
        in_specs=[idx_spec, row_spec, any_spec],
        out_specs=any_spec,
        out_shape=jax.ShapeDtypeStruct(dst_init.shape, F32),
        scratch_shapes=scratch,
        input_output_aliases={2: 0},
        compiler_params=params,
    )(idx3, src, dst_init)


def _moe_tile_kernel(elo_ref, ehi_ref, ok_ref, x_ref, wr_ref, br_ref,
                     wg0, wu0, wd0, wg1, wu1, wd1, lg_ref, lb_ref, o_ref):
    t = pl.program_id(0)

    @pl.when(ok_ref[t] == 0)
    def _():
        o_ref[...] = jnp.zeros(o_ref.shape, F32)

    @pl.when(ok_ref[t] != 0)
    def _():
        x = _from_slabs(x_ref)
        gates = _route(x, wr_ref[...], br_ref[...])[0]
        lane = lax.broadcasted_iota(jnp.int32, gates.shape, 1)
        xb = x.astype(BF16)
        y = None
        for e, wg, wu, wd in ((elo_ref[t], wg0, wu0, wd0), (ehi_ref[t], wg1, wu1, wd1)):
            ge = jnp.sum(jnp.where(lane == e + ROUTE_OFF, gates, 0.0), axis=1, keepdims=True)
            h = _silu(_dot(xb, wg[0])) * _dot(xb, wu[0]) * ge
            ye = _dot(h.astype(BF16), wd[0])
            y = ye if y is None else y + ye
        _to_slabs(_layer_norm(DN_ALPHA * x + y, lg_ref[...], lb_ref[...]), o_ref)


def _moe_sorted_ln(x, w_route, b_route, wg, wu, wd, g, b, tm_router):
    n, d = x.shape
    tile = MOE_TILE
    n_tiles = n // tile + MOE_CLASSES
    cls, rank, cnt = _router_sort(x, w_route, b_route, tm_router)
    counts = cnt[0, :MOE_CLASSES].astype(jnp.int32)
    tiles_c = (counts + tile - 1) // tile
    tile_end = jnp.cumsum(tiles_c)
    tile_start = tile_end - tiles_c
    dest = jnp.take(tile_start, cls[:, 0]) * tile + rank[:, 0]
    tids = jnp.arange(n_tiles, dtype=jnp.int32)
    n_used = tile_end[-1]
    t_cls = jnp.searchsorted(tile_end, jnp.minimum(tids, n_used - 1), side='right').astype(jnp.int32)
    t_elo = jnp.take(jnp.array(CLASS_ELO, jnp.int32), t_cls)
    t_ehi = jnp.take(jnp.array(CLASS_EHI, jnp.int32), t_cls)
    t_ok = (tids < n_used).astype(jnp.int32)

    xs = _permute_rows(x, dest, jnp.zeros((n_tiles * tile, ROW_SLABS, LANES), F32), gather=False)
    wspec = lambda tab_pos: pl.BlockSpec(
        (1,) + wg.shape[1:], lambda t, elo, ehi, ok: ((elo, ehi)[tab_pos][t], 0, 0))
    dspec = lambda tab_pos: pl.BlockSpec(
        (1,) + wd.shape[1:], lambda t, elo, ehi, ok: ((elo, ehi)[tab_pos][t], 0, 0))
    const = lambda shape: pl.BlockSpec(shape, lambda t, elo, ehi, ok: (0,) * len(shape))
    row_spec = pl.BlockSpec((tile, ROW_SLABS, LANES), lambda t, elo, ehi, ok: (t, 0, 0))
    grid_spec = pltpu.PrefetchScalarGridSpec(
        num_scalar_prefetch=3,
        grid=(n_tiles,),
        in_specs=[row_spec, const(w_route.shape), const(b_route.shape),
                  wspec(0), wspec(0), dspec(0), wspec(1), wspec(1), dspec(1),
                  const((1, d)), const((1, d))],
        out_specs=row_spec,
    )
    ys = pl.pallas_call(
        _moe_tile_kernel,
        grid_spec=grid_spec,
        out_shape=jax.ShapeDtypeStruct(xs.shape, F32),
        compiler_params=_cparams("arbitrary"),
    )(t_elo, t_ehi, t_ok, xs, w_route, b_route, wg, wu, wd, wg, wu, wd, g.reshape(1, d), b.reshape(1, d))
    return _permute_rows(ys, dest, None, gather=True)


def _pad_time(a, n_seq, t_len, t_pad):
    w = a.shape[1]
    a = a.reshape(n_seq, t_len, w)
    a = jnp.pad(a, ((0, 0), (0, t_pad - t_len), (0, 0)))
    return a.reshape(n_seq * t_pad, w)


def _forward(x_prompt, x_sample, cache_k, cache_v, state_hgrn, state_s5_re, state_s5_im, page_table,
             w_in_ab, w_out_ab, hg_lb_param, hg_norm_g, s5_lam_re, s5_lam_im, s5_log_step,
             s5_b_re, s5_b_im, s5_c_re, s5_c_im, s5_d, s5_w_glu,
             w_in_c, w_out_c, da_lam_q1, da_lam_k1, da_lam_q2, da_lam_k2, da_subln_g,
             moe_w_group, moe_b_group, moe_w_expert, moe_b_expert, moe_w_gate, moe_w_up, moe_w_down,
             ln_g, ln_b):
    bp, tp, d = x_prompt.shape
    bs, ts, _ = x_sample.shape
    ts_pad = SUBLANES
    n_pool = cache_k.shape[1]
    xp = x_prompt.reshape(bp * tp, d)
    xs = x_sample.reshape(bs * ts, d)
    tm_p = min(512, bp * tp)
    tq = min(512, tp)
    s5_tb = min(256, tp)

    slopes = 2.0 ** (-8.0 * jnp.arange(1, DA_HEADS + 1, dtype=F32) / DA_HEADS)
    sm = jax.nn.softmax(hg_lb_param.astype(F32), axis=0)
    lbs = jnp.cumsum(sm, axis=0) - sm[:1]
    ck = cache_k.reshape(cache_k.shape[0], n_pool, PAGE_SIZE * DA_HEADS, 2 * DA_DK)
    cv = cache_v.reshape(cache_v.shape[0], n_pool, PAGE_SIZE * DA_HEADS, DA_DV)
    zeros_hg = jnp.zeros((bp, HG_HEADS, HG_DK, HG_DK), F32)
    zeros_s5 = jnp.zeros((bp, 1, S5_N), F32)

    new_hg_p, new_re_p, new_im_p = [], [], []
    kv_bufs_p = None
    new_k_s, new_v_s, new_hg_s, new_re_s, new_im_s = [], [], [], [], []
    for i in range(DEPTH):
        j = i // 2
        if i % 2 == 0:
            w_in = w_in_ab[j].astype(BF16)
            w_out = w_out_ab[j].astype(BF16)
            s5w = _s5_weights(s5_lam_re[j], s5_lam_im[j], s5_log_step[j], s5_b_re[j], s5_b_im[j],
                              s5_c_re[j], s5_c_im[j], s5_d[j], s5_w_glu[j])
            zp = _matmul(xp, w_in, tm_p)
            oa_p, hg_p = _hgrn(zp, lbs[j], hg_norm_g[j], zeros_hg, bp, tp, math.gcd(tp, HG_CHUNK), tp)
            ob_p, re_p, im_p = _s5(zp, s5w, zeros_s5, zeros_s5, bp, tp, s5_tb, False, SUBLANES)
            xp = _proj_ln([oa_p, ob_p], [w_out[:HG_WIDTH], w_out[HG_WIDTH:]], xp,
                          ln_g[i, 0], ln_b[i, 0], tm_p)
            zs = _pad_time(_matmul(xs, w_in, bs * ts), bs, ts, ts_pad)
            s0t = jnp.swapaxes(state_hgrn[j], -1, -2)
            oa_s, hg_s = _hgrn(zs, lbs[j], hg_norm_g[j], s0t, bs, ts_pad, ts_pad, ts)
            ob_s, re_s, im_s = _s5(zs, s5w, state_s5_re[j].reshape(bs, 1, S5_N),
                                   state_s5_im[j].reshape(bs, 1, S5_N), bs, ts_pad, ts_pad, True, ts)
            unpad = lambda a: a.reshape(bs, ts_pad, -1)[:, :ts].reshape(bs * ts, -1)
            xs = _proj_ln([unpad(oa_s), unpad(ob_s)], [w_out[:HG_WIDTH], w_out[HG_WIDTH:]], xs,
                          ln_g[i, 0], ln_b[i, 0], bs * ts)
            new_hg_p.append(jnp.swapaxes(hg_p, -1, -2))
            new_hg_s.append(jnp.swapaxes(hg_s, -1, -2))
            new_re_p.append(re_p.reshape(bp, S5_GROUPS, S5_STATE))
            new_im_p.append(im_p.reshape(bp, S5_GROUPS, S5_STATE))
            new_re_s.append(re_s.reshape(bs, S5_GROUPS, S5_STATE))
            new_im_s.append(im_s.reshape(bs, S5_GROUPS, S5_STATE))
        else:
            w_in = w_in_c[j].astype(BF16)
            w_out = w_out_c[j].astype(BF16)
            lam_init = 0.8 - 0.6 * math.exp(-0.3 * i)
            lam = (jnp.exp(jnp.sum(da_lam_q1[j].astype(F32) * da_lam_k1[j].astype(F32)))
                   - jnp.exp(jnp.sum(da_lam_q2[j].astype(F32) * da_lam_k2[j].astype(F32))) + lam_init)
            qp, kbuf_p, vbuf_p, vt_p = _qkv_proj(xp, w_in, kv_bufs_p, j, DEPTH // 2, tm_p // 2)
            kv_bufs_p = (kbuf_p, vbuf_p)
            oc_p = _pattn(qp, kbuf_p, vt_p, j, slopes, lam, da_subln_g[j], bp, tp, tq, lam_init)
            xp = _proj_ln([oc_p], [w_out], xp, ln_g[i, 0], ln_b[i, 0], tm_p)

            zs = _matmul(xs, w_in, bs * ts)
            rows = lambda a: a.reshape(bs, ts * DA_HEADS, DA_DV)
            zq_s, zk_s, zv_s = zs[:, :DA_QK], zs[:, DA_QK:2 * DA_QK], zs[:, 2 * DA_QK:]
            oc_s = _sattn(rows(zq_s), rows(zk_s), rows(zv_s), ck, cv, page_table, j,
                          slopes, lam, da_subln_g[j], lam_init)
            new_k_s.append(zk_s.reshape(bs, ts, DA_HEADS, 2 * DA_DK))
            new_v_s.append(zv_s.reshape(bs, ts, DA_HEADS, DA_DV))
            xs = _proj_ln([oc_s.reshape(bs * ts, DA_HEADS * DA_DV)], [w_out], xs,
                          ln_g[i, 0], ln_b[i, 0], bs * ts)

        w_route = jnp.pad(jnp.concatenate([moe_w_group[i], moe_w_expert[i]], axis=1).astype(F32),
                          ((0, 0), (0, LANES - MOE_GROUPS - MOE_EXPERTS)))
        b_route = jnp.pad(jnp.concatenate([moe_b_group[i], moe_b_expert[i]]).astype(F32),
                          (0, LANES - MOE_GROUPS - MOE_EXPERTS)).reshape(1, LANES)
        wg = moe_w_gate[i].astype(BF16)
        wu = moe_w_up[i].astype(BF16)
        wd = moe_w_down[i].astype(BF16)
        xp = _moe_sorted_ln(xp, w_route, b_route, wg, wu, wd, ln_g[i, 1], ln_b[i, 1], tm_p)
        xs = _moe_ln(xs, _router(xs, w_route, b_route, bs * ts), wg, wu, wd,
                     ln_g[i, 1], ln_b[i, 1], bs * ts)

    return (xp.reshape(bp, tp, d), xs.reshape(bs, ts, d),
            kv_bufs_p[0].reshape(DEPTH // 2, bp, tp, DA_HEADS, 2 * DA_DK),
            kv_bufs_p[1].reshape(DEPTH // 2, bp, tp, DA_HEADS, DA_DV), jnp.stack(new_hg_p),
            jnp.stack(new_re_p), jnp.stack(new_im_p),
            jnp.stack(new_k_s), jnp.stack(new_v_s), jnp.stack(new_hg_s),
            jnp.stack(new_re_s), jnp.stack(new_im_s))


_forward_jit = jax.jit(_forward)


def kernel(x_prompt, x_sample, cache_k, cache_v, state_hgrn, state_s5_re, state_s5_im, page_table, w_in_ab, w_out_ab, hg_lb_param, hg_norm_g, s5_lam_re, s5_lam_im, s5_log_step, s5_b_re, s5_b_im, s5_c_re, s5_c_im, s5_d, s5_w_glu, w_in_c, w_out_c, da_lam_q1, da_lam_k1, da_lam_q2, da_lam_k2, da_subln_g, moe_w_group, moe_b_group, moe_w_expert, moe_b_expert, moe_w_gate, moe_w_up, moe_w_down, ln_g, ln_b):
    return _forward_jit(x_prompt, x_sample, cache_k, cache_v, state_hgrn, state_s5_re, state_s5_im, page_table, w_in_ab, w_out_ab, hg_lb_param, hg_norm_g, s5_lam_re, s5_lam_im, s5_log_step, s5_b_re, s5_b_im, s5_c_re, s5_c_im, s5_d, s5_w_glu, w_in_c, w_out_c, da_lam_q1, da_lam_k1, da_lam_q2, da_lam_k2, da_subln_g, moe_w_group, moe_b_group, moe_w_expert, moe_b_expert, moe_w_gate, moe_w_up, moe_w_down, ln_g, ln_b)
```

```python
import functools
import math

import jax
import jax.numpy as jnp
from jax import lax
from jax.experimental import pallas as pl
from jax.experimental.pallas import tpu as pltpu

F32 = jnp.float32
BF16 = jnp.bfloat16

D_MODEL = 1024
DEPTH = 4
HG_HEADS = 4
HG_DK = 128
HG_WIDTH = HG_HEADS * HG_DK
S5_WIDTH = 512
S5_GROUPS = 32
S5_GROUP = 16
S5_STATE = 64
S5_N = S5_GROUPS * S5_STATE
DA_HEADS = 8
DA_DK = 64
DA_DV = 128
DA_QK = DA_HEADS * 2 * DA_DK
MOE_GROUPS = 4
MOE_PER_GROUP = 4
MOE_EXPERTS = 16
MOE_FF = 512
PAGE_SIZE = 128
DN_ALPHA = (2.0 * DEPTH) ** 0.25
LN_EPS = 1e-5
NEG_BIG = -1e30
LB_FLOOR = 1e-30

SUBLANES = 8
LANES = 128
VMEM_LIMIT = 48 * 1024 * 1024

HG_CHUNK = 64
ROUTE_OFF = MOE_GROUPS


def _cparams(*sem):
    return pltpu.CompilerParams(dimension_semantics=sem, vmem_limit_bytes=VMEM_LIMIT)


def _dot(a, b):
    return jnp.dot(a, b, preferred_element_type=F32)


def _dot_nt(a, b):
    return lax.dot_general(a, b, (((1,), (1,)), ((), ())), preferred_element_type=F32)


def _dot_tn(a, b):
    return lax.dot_general(a, b, (((0,), (0,)), ((), ())), preferred_element_type=F32)


def _sigmoid(x):
    return 1.0 / (1.0 + jnp.exp(-x))


def _silu(x):
    return x * _sigmoid(x)


def _expm1(x):
    u = jnp.exp(x)
    um1 = u - 1.0
    r = jnp.where(u == 1.0, x, um1 * x / jnp.log(u))
    return jnp.where(um1 == -1.0, -1.0, r)


def _layer_norm(y, g, b):
    mu = jnp.mean(y, axis=-1, keepdims=True)
    yc = y - mu
    var = jnp.mean(yc * yc, axis=-1, keepdims=True)
    return yc * lax.rsqrt(var + LN_EPS) * g + b


def _mm_kernel(x_ref, w_ref, o_ref):
    o_ref[...] = _dot(x_ref[...].astype(BF16), w_ref[...])


def _matmul(x, w, tm):
    m, k = x.shape
    n = w.shape[1]
    tm = min(tm, m)
    return pl.pallas_call(
        _mm_kernel,
        grid=(m // tm,),
        in_specs=[pl.BlockSpec((tm, k), lambda i: (i, 0)),
                  pl.BlockSpec((k, n), lambda i: (0, 0))],
        out_specs=pl.BlockSpec((tm, n), lambda i: (i, 0)),
        out_shape=jax.ShapeDtypeStruct((m, n), F32),
        compiler_params=_cparams("arbitrary"),
    )(x, w)


def _proj_ln_kernel(*refs, n_lhs):
    a_refs = refs[:n_lhs]
    w_refs = refs[n_lhs:2 * n_lhs]
    x_ref, g_ref, b_ref, o_ref = refs[2 * n_lhs:]
    acc = _dot(a_refs[0][...].astype(BF16), w_refs[0][...])
    for a_ref, w_ref in zip(a_refs[1:], w_refs[1:]):
        acc = acc + _dot(a_ref[...].astype(BF16), w_ref[...])
    o_ref[...] = _layer_norm(DN_ALPHA * x_ref[...] + acc, g_ref[...], b_ref[...])


def _proj_ln(lhs, ws, x, g, b, tm):
    m, d = x.shape
    tm = min(tm, m)
    n_lhs = len(lhs)
    in_specs = [pl.BlockSpec((tm, a.shape[1]), lambda i: (i, 0)) for a in lhs]
    in_specs += [pl.BlockSpec(w.shape, lambda i: (0, 0)) for w in ws]
    in_specs += [pl.BlockSpec((tm, d), lambda i: (i, 0)),
                 pl.BlockSpec((1, d), lambda i: (0, 0)),
                 pl.BlockSpec((1, d), lambda i: (0, 0))]
    return pl.pallas_call(
        functools.partial(_proj_ln_kernel, n_lhs=n_lhs),
        grid=(m // tm,),
        in_specs=in_specs,
        out_specs=pl.BlockSpec((tm, d), lambda i: (i, 0)),
        out_shape=jax.ShapeDtypeStruct((m, d), F32),
        compiler_params=_cparams("arbitrary"),
    )(*lhs, *ws, x, g.reshape(1, d), b.reshape(1, d))


def _cumsum_rows(x, row):
    n = x.shape[0]
    s = 1
    while s < n:
        x = x + jnp.where(row >= s, pltpu.roll(x, s, axis=0), 0.0)
        s *= 2
    return x


def _hgrn_kernel(zq_ref, zf_ref, zi_ref, zg_ref, lb_ref, ng_ref, s0_ref,
                 o_ref, st_ref, st_scr, *, chunk, t_valid, n_chunks):
    c = pl.program_id(1)
    nb = chunk // SUBLANES

    @pl.when(c == 0)
    def _():
        st_scr[...] = s0_ref[0]

    row = lax.broadcasted_iota(jnp.int32, (chunk, HG_DK), 0)
    row8 = lax.broadcasted_iota(jnp.int32, (SUBLANES, HG_DK), 0)
    rowa = lax.broadcasted_iota(jnp.int32, (SUBLANES, chunk), 0)
    cola = lax.broadcasted_iota(jnp.int32, (SUBLANES, chunk), 1)

    for h in range(HG_HEADS):
        sl = slice(h * HG_DK, (h + 1) * HG_DK)
        zf = zf_ref[:, sl]
        zq = zq_ref[:, sl]
        zg = zg_ref[:, sl]
        v = zi_ref[:, sl]
        lb = lb_ref[:, sl]
        log_lb = jnp.log(jnp.maximum(lb, LB_FLOOR))
        log_sig = jnp.minimum(zf, 0.0) - jnp.log1p(jnp.exp(-jnp.abs(zf)))
        t = jnp.log1p(-lb) + log_sig
        logf = jnp.maximum(log_lb, t) + jnp.log1p(jnp.exp(-jnp.abs(log_lb - t)))
        k = -_expm1(logf)
        q = _silu(zq)
        if t_valid < chunk:
            valid = row < t_valid
            logf = jnp.where(valid, logf, 0.0)
            k = jnp.where(valid, k, 0.0)
        b = _cumsum_rows(logf, row)
        b_last = b[chunk - 1:chunk]
        st = st_scr[h]
        vb = v.astype(BF16)

        o = _dot_nt((q * jnp.exp(b)).astype(BF16), st.astype(BF16))
        blocks = []
        for i in range(nb):
            rs = slice(i * SUBLANES, (i + 1) * SUBLANES)
            qi, ki, bi = q[rs], k[rs], b[rs]
            if i > 0:
                r = b[i * SUBLANES - 1:i * SUBLANES]
                qt = qi * jnp.exp(bi - r)
                kt = k * jnp.exp(jnp.minimum(r - b, 0.0))
                att = _dot_nt(qt.astype(BF16), kt.astype(BF16))
                att = jnp.where(cola < i * SUBLANES, att, 0.0)
            else:
                att = jnp.zeros((SUBLANES, chunk), F32)
            for d in range(SUBLANES):
                if d == 0:
                    prod = qi * ki
                else:
                    kd = pltpu.roll(ki, d, axis=0)
                    bd = pltpu.roll(bi, d, axis=0)
                    dec = jnp.exp(jnp.where(row8 >= d, bi - bd, NEG_BIG))
                    prod = qi * kd * dec
                a = jnp.sum(prod, axis=1, keepdims=True)
                att = att + jnp.where(cola == rowa + (i * SUBLANES - d), a, 0.0)
            blocks.append(att)
        att = blocks[0] if nb == 1 else jnp.concatenate(blocks, axis=0)
        o = o + _dot(att.astype(BF16), vb)

        ke = k * jnp.exp(b_last - b)
        st_scr[h] = jnp.exp(b_last) * st + _dot_tn(vb, ke.astype(BF16))

        on = o * lax.rsqrt(jnp.mean(o * o, axis=-1, keepdims=True) + LN_EPS) * ng_ref[:, sl]
        o_ref[:, sl] = on * _silu(zg)

    @pl.when(c == n_chunks - 1)
    def _():
        st_ref[0] = st_scr[...]


def _hgrn(z, lb, norm_g, s0t, n_seq, t_len, chunk, t_valid):
    n_chunks = t_len // chunk
    zspec = lambda cb: pl.BlockSpec((chunk, HG_WIDTH), lambda b, c: (b * n_chunks + c, cb))
    pspec = pl.BlockSpec((1, HG_WIDTH), lambda b, c: (0, 0))
    sspec = pl.BlockSpec((1, HG_HEADS, HG_DK, HG_DK), lambda b, c: (b, 0, 0, 0))
    return pl.pallas_call(
        functools.partial(_hgrn_kernel, chunk=chunk, t_valid=t_valid, n_chunks=n_chunks),
        grid=(n_seq, n_chunks),
        in_specs=[zspec(0), zspec(1), zspec(2), zspec(3), pspec, pspec, sspec],
        out_specs=[pl.BlockSpec((chunk, HG_WIDTH), lambda b, c: (b * n_chunks + c, 0)), sspec],
        out_shape=[jax.ShapeDtypeStruct((n_seq * t_len, HG_WIDTH), F32),
                   jax.ShapeDtypeStruct(s0t.shape, F32)],
        scratch_shapes=[pltpu.VMEM((HG_HEADS, HG_DK, HG_DK), F32)],
        compiler_params=_cparams("arbitrary", "arbitrary"),
    )(z, z, z, z, lb.reshape(1, HG_WIDTH), norm_g.reshape(1, HG_WIDTH), s0t)


S5_LANE_CHUNK = 512


def _s5_kernel(u_ref, bre_ref, bim_ref, cre_ref, cim_ref, coef_ref, pre_ref, pim_ref, mre_ref, mim_ref,
               d_ref, wglu_ref, h0re_ref, h0im_ref,
               o_ref, hre_ref, him_ref, xr_scr, xi_scr, c_scr,
               *, tb, n_tb, per_group, t_valid):
    tstep = pl.program_id(1)
    u = u_ref[...]
    ub = u.astype(BF16)
    n_chunks = S5_N // S5_LANE_CHUNK
    in_w = S5_WIDTH // n_chunks

    if not per_group:
        @pl.when(tstep == 0)
        def _():
            c_scr[0:1] = h0re_ref[0]
            c_scr[1:2] = h0im_ref[0]

    y_parts = []
    for lc in range(n_chunks):
        ls = slice(lc * S5_LANE_CHUNK, (lc + 1) * S5_LANE_CHUNK)
        ubj = ub[:, lc * in_w:(lc + 1) * in_w]
        xr_raw = _dot(ubj, bre_ref[lc])
        xi_raw = _dot(ubj, bim_ref[lc])
        cfr = coef_ref[0:1, ls]
        cfi = coef_ref[1:2, ls]
        xr_scr[:, ls] = cfr * xr_raw - cfi * xi_raw
        xi_scr[:, ls] = cfr * xi_raw + cfi * xr_raw
        pr = pre_ref[:, ls]
        pi = pim_ref[:, ls]

        def body(g, carry, ls=ls, pr=pr, pi=pi):
            if per_group:
                cr = h0re_ref[g][:, ls]
                ci = h0im_ref[g][:, ls]
            else:
                cr, ci = carry
            rs = pl.ds(pl.multiple_of(g * SUBLANES, SUBLANES), SUBLANES)
            xr = xr_scr[rs, ls]
            xi = xi_scr[rs, ls]
            for k in range(3):
                ar = mre_ref[k * SUBLANES:(k + 1) * SUBLANES, ls]
                ai = mim_ref[k * SUBLANES:(k + 1) * SUBLANES, ls]
                sr = pltpu.roll(xr, 1 << k, axis=0)
                si = pltpu.roll(xi, 1 << k, axis=0)
                xr, xi = xr + (ar * sr - ai * si), xi + (ar * si + ai * sr)
            hr = xr + (pr * cr - pi * ci)
            hi = xi + (pr * ci + pi * cr)
            xr_scr[rs, ls] = hr
            xi_scr[rs, ls] = hi
            if per_group:
                hre_ref[g, :, ls] = hr[t_valid - 1:t_valid]
                him_ref[g, :, ls] = hi[t_valid - 1:t_valid]
            return hr[SUBLANES - 1:SUBLANES], hi[SUBLANES - 1:SUBLANES]

        carry0 = (c_scr[0:1, ls], c_scr[1:2, ls])
        cr, ci = lax.fori_loop(0, tb // SUBLANES, body, carry0)
        if not per_group:
            c_scr[0:1, ls] = cr
            c_scr[1:2, ls] = ci
        y_parts.append(_dot(xr_scr[:, ls].astype(BF16), cre_ref[lc])
                       - _dot(xi_scr[:, ls].astype(BF16), cim_ref[lc]))

    y = jnp.concatenate(y_parts, axis=1) + d_ref[...] * u
    y = y * (0.5 * (1.0 + jnp.tanh(math.sqrt(2.0 / math.pi) * (y + 0.044715 * (y * y * y)))))
    gl = _dot(y.astype(BF16), wglu_ref[...])
    o_ref[...] = gl[:, :S5_WIDTH] * _sigmoid(gl[:, S5_WIDTH:])

    if not per_group:
        @pl.when(tstep == n_tb - 1)
        def _():
            hre_ref[0] = c_scr[0:1]
            him_ref[0] = c_scr[1:2]


def _s5(z, wts, h0re, h0im, n_seq, t_len, tb, per_group, t_valid):
    bre, bim, cre, cim, coef, pre, pim, mre, mim, dvec, wglu = wts
    if per_group:
        grid = (1, 1)
        n_tb = 1
        tb = z.shape[0]
        hspec = pl.BlockSpec(h0re.shape, lambda b, t: (0, 0, 0))
    else:
        n_tb = t_len // tb
        grid = (n_seq, n_tb)
        hspec = pl.BlockSpec((1, 1, S5_N), lambda b, t: (b, 0, 0))
    ucol = (4 * HG_WIDTH) // S5_WIDTH
    full = lambda a: pl.BlockSpec(a.shape, lambda b, t: (0,) * a.ndim)
    return pl.pallas_call(
        functools.partial(_s5_kernel, tb=tb, n_tb=n_tb, per_group=per_group, t_valid=t_valid),
        grid=grid,
        in_specs=[pl.BlockSpec((tb, S5_WIDTH), lambda b, t: (b * n_tb + t, ucol)),
                  full(bre), full(bim), full(cre), full(cim), full(coef), full(pre), full(pim),
                  full(mre), full(mim), full(dvec), full(wglu), hspec, hspec],
        out_specs=[pl.BlockSpec((tb, S5_WIDTH), lambda b, t: (b * n_tb + t, 0)), hspec, hspec],
        out_shape=[jax.ShapeDtypeStruct((z.shape[0], S5_WIDTH), F32),
                   jax.ShapeDtypeStruct(h0re.shape, F32),
                   jax.ShapeDtypeStruct(h0im.shape, F32)],
        scratch_shapes=[pltpu.VMEM((tb, S5_N), F32), pltpu.VMEM((tb, S5_N), F32),
                        pltpu.VMEM((2, S5_N), F32)],
        compiler_params=_cparams("arbitrary", "arbitrary"),
    )(z, bre, bim, cre, cim, coef, pre, pim, mre, mim, dvec, wglu, h0re, h0im)


def _s5_weights(lam_re, lam_im, log_step, b_re, b_im, c_re, c_im, d, w_glu):
    lr, li = lam_re.astype(F32), lam_im.astype(F32)
    dt = jnp.exp(log_step.astype(F32))[:, None]
    mag = jnp.exp(lr * dt)
    ab_re, ab_im = mag * jnp.cos(li * dt), mag * jnp.sin(li * dt)
    den = lr * lr + li * li
    nr, ni = ab_re - 1.0, ab_im
    coef_re = (nr * lr + ni * li) / den
    coef_im = (ni * lr - nr * li) / den
    ar, ai = ab_re.reshape(1, S5_N), ab_im.reshape(1, S5_N)
    pr, pi = [ar], [ai]
    for _ in range(SUBLANES - 1):
        pr, pi = pr + [pr[-1] * ar - pi[-1] * ai], pi + [pr[-1] * ai + pi[-1] * ar]
    pre = jnp.concatenate(pr, axis=0)
    pim = jnp.concatenate(pi, axis=0)
    coef = jnp.concatenate([coef_re.reshape(1, S5_N), coef_im.reshape(1, S5_N)], axis=0)
    rows = jnp.arange(SUBLANES)[:, None]
    masked = lambda p: jnp.concatenate(
        [jnp.where(rows >= (1 << k), p[(1 << k) - 1:(1 << k)], 0.0) for k in range(3)], axis=0)
    n_chunks = S5_N // S5_LANE_CHUNK
    gpc = S5_GROUPS // n_chunks
    eye = jnp.eye(gpc, dtype=F32)
    bd = lambda w: jnp.einsum('jgpc,gh->jgchp', w.reshape(n_chunks, gpc, S5_STATE, S5_GROUP), eye).reshape(
        n_chunks, gpc * S5_GROUP, S5_LANE_CHUNK).astype(BF16)
    cd = lambda w: jnp.einsum('jgcp,gh->jgphc', w.reshape(n_chunks, gpc, S5_GROUP, S5_STATE), eye).reshape(
        n_chunks, S5_LANE_CHUNK, gpc * S5_GROUP).astype(BF16)
    return (bd(b_re), bd(b_im), cd(c_re), cd(c_im), coef, pre, pim, masked(pre), masked(pim),
            d.reshape(1, S5_WIDTH).astype(F32), w_glu.astype(BF16))


PATTN_STRIP = 256
ALIBI_SPLIT_BITS = 7


def _qkv_kernel(*refs):
    x_ref, w_ref, wvt_ref = refs[:3]
    q_ref, k_ref, v_ref, vt_ref = refs[-4:]
    xb = x_ref[...].astype(BF16)
    z = _dot(xb, w_ref[...])
    q_ref[...] = z[:, :DA_QK]
    k_ref[0] = z[:, DA_QK:2 * DA_QK]
    v_ref[0] = z[:, 2 * DA_QK:]
    for slot in range(1, k_ref.shape[0]):
        k_ref[slot] = jnp.zeros(k_ref.shape[1:], F32)
        v_ref[slot] = jnp.zeros(v_ref.shape[1:], F32)
    vt_ref[...] = _dot_nt(wvt_ref[...], xb).astype(BF16)


def _qkv_proj(x, w, kv_bufs, layer, n_layers, tm):
    m, d = x.shape
    tm = min(tm, m)
    wvt = jnp.transpose(w[:, 2 * DA_QK:])
    in_specs = [pl.BlockSpec((tm, d), lambda i: (i, 0)), pl.BlockSpec(w.shape, lambda i: (0, 0)),
                pl.BlockSpec(wvt.shape, lambda i: (0, 0))]
    args = [x, w, wvt]
    aliases = {}
    if kv_bufs is not None:
        in_specs += [pl.BlockSpec(memory_space=pl.ANY)] * 2
        args += list(kv_bufs)
        aliases = {3: 1, 4: 2}
        kv_spec = pl.BlockSpec((1, tm, DA_QK), lambda i: (layer, i, 0))
    else:
        assert layer == 0
        kv_spec = pl.BlockSpec((n_layers, tm, DA_QK), lambda i: (0, i, 0))
    kv_shape = jax.ShapeDtypeStruct((n_layers, m, DA_QK), F32)
    return pl.pallas_call(
        _qkv_kernel,
        grid=(m // tm,),
        in_specs=in_specs,
        out_specs=[pl.BlockSpec((tm, DA_QK), lambda i: (i, 0)), kv_spec, kv_spec,
                   pl.BlockSpec((DA_HEADS * DA_DV, tm), lambda i: (0, i))],
        out_shape=[jax.ShapeDtypeStruct((m, DA_QK), F32), kv_shape, kv_shape,
                   jax.ShapeDtypeStruct((DA_HEADS * DA_DV, m), BF16)],
        input_output_aliases=aliases,
        compiler_params=_cparams("arbitrary"),
    )(*args)


def _pattn_kernel(qi_ref, ki_ref, q_ref, k_ref, vt_ref, slope_ref, lam_ref, g_ref, o_ref,
                  m0, l0, a0, m1, l1, a1, *, tq, lam_init):
    step = pl.program_id(2)
    qi = qi_ref[step]
    ki = ki_ref[step]
    stats = ((m0, l0, a0), (m1, l1, a1))

    @pl.when(ki == 0)
    def _():
        for m, l, a in stats:
            m[...] = jnp.full(m.shape, -jnp.inf, F32)
            l[...] = jnp.zeros(l.shape, F32)
            a[...] = jnp.zeros(a.shape, F32)

    q = q_ref[...] * (DA_DK ** -0.5)
    k = k_ref[0]
    lane = lax.broadcasted_iota(jnp.int32, q.shape, 1)
    kpos = ki * tq + lax.broadcasted_iota(jnp.int32, k.shape, 0)
    lo_mask = (1 << ALIBI_SPLIT_BITS) - 1
    slope = slope_ref[0]
    b_hi = slope * (kpos - jnp.bitwise_and(kpos, lo_mask)).astype(F32)
    b_lo = slope * jnp.bitwise_and(kpos, lo_mask).astype(F32)
    qs, ks = [], []
    for half in range(2):
        in_half = (lane >= DA_DK) if half else (lane < DA_DK)
        spare = 0 if half else DA_DK
        qs.append(jnp.where(in_half, q, jnp.where((lane == spare) | (lane == spare + 1), 1.0, 0.0))
                  .astype(BF16))
        ks.append(jnp.where(in_half, k, jnp.where(lane == spare, b_hi,
                                                  jnp.where(lane == spare + 1, b_lo, 0.0)))
                  .astype(BF16))
    vt = vt_ref[...]

    def update(masked):
        work = []
        strip = min(PATTN_STRIP, tq)
        for c0 in range(0, tq, strip):
            cs = slice(c0, c0 + strip)
            for qh, kh, stat in zip(qs, ks, stats):
                work.append((c0, cs, stat, _dot_nt(kh, qh[cs])))
        probs = []
        for c0, cs, (m, l, a), st in work:
            if masked:
                krow = lax.broadcasted_iota(jnp.int32, (tq, strip), 0)
                qcol = lax.broadcasted_iota(jnp.int32, (tq, strip), 1) + c0
                st = jnp.where(krow <= qcol, st, NEG_BIG)
            m_old = m[:, cs]
            m_new = jnp.maximum(m_old, jnp.max(st, axis=0, keepdims=True))
            alpha = jnp.exp(m_old - m_new)
            pt = jnp.exp(st - m_new)
            l[:, cs] = alpha * l[:, cs] + jnp.sum(pt, axis=0, keepdims=True)
            m[:, cs] = m_new
            probs.append((cs, a, alpha, pt.astype(BF16)))
        for cs, a, alpha, pb in probs:
            a[:, cs] = alpha * a[:, cs] + _dot(vt, pb)

    @pl.when(ki < qi)
    def _():
        update(False)

    @pl.when(ki == qi)
    def _():
        update(True)
        ot = a0[...] / l0[...] - lam_ref[...] * (a1[...] / l1[...])
        ot = ot * lax.rsqrt(jnp.mean(ot * ot, axis=0, keepdims=True) + LN_EPS)
        o_ref[...] = jnp.transpose(ot) * (g_ref[...] * (1.0 - lam_init))


def _pattn(q, kbuf, vt, layer, slopes, lam, subln_g, n_seq, t_len, tq, lam_init):
    nq = t_len // tq
    pairs = [(i, j) for i in range(nq) for j in range(i + 1)]
    qi_tab = jnp.array([p[0] for p in pairs], jnp.int32)
    ki_tab = jnp.array([p[1] for p in pairs], jnp.int32)
    q_spec = pl.BlockSpec((tq, DA_DV), lambda b, h, s, qt, kt: (b * nq + qt[s], h))
    grid_spec = pltpu.PrefetchScalarGridSpec(
        num_scalar_prefetch=2,
        grid=(n_seq, DA_HEADS, len(pairs)),
        in_specs=[q_spec,
                  pl.BlockSpec((1, tq, DA_DV), lambda b, h, s, qt, kt: (layer, b * nq + kt[s], h)),
                  pl.BlockSpec((DA_DV, tq), lambda b, h, s, qt, kt: (h, b * nq + kt[s])),
                  pl.BlockSpec((1, 1, 1), lambda b, h, s, qt, kt: (h, 0, 0)),
                  pl.BlockSpec((1, 1), lambda b, h, s, qt, kt: (0, 0)),
                  pl.BlockSpec((1, DA_DV), lambda b, h, s, qt, kt: (0, 0))],
        out_specs=q_spec,
        scratch_shapes=[pltpu.VMEM((1, tq), F32), pltpu.VMEM((1, tq), F32), pltpu.VMEM((DA_DV, tq), F32)] * 2,
    )
    return pl.pallas_call(
        functools.partial(_pattn_kernel, tq=tq, lam_init=lam_init),
        grid_spec=grid_spec,
        out_shape=jax.ShapeDtypeStruct((n_seq * t_len, DA_HEADS * DA_DV), F32),
        compiler_params=_cparams("arbitrary", "arbitrary", "arbitrary"),
    )(qi_tab, ki_tab, q, kbuf, vt, slopes.reshape(DA_HEADS, 1, 1), lam.reshape(1, 1),
      subln_g.reshape(1, DA_DV))


MAX_PAGES_PER_STEP = 8


def _sattn_kernel(pt_ref, q_ref, kn_ref, vn_ref, cmat_ref, slope_ref, lam_ref, g_ref, *rest,
                  n_pages, pps, t_new, lam_init):
    kp_refs = rest[:pps]
    vp_refs = rest[pps:2 * pps]
    o_ref, m_scr, l_scr, a_scr = rest[2 * pps:]
    p = pl.program_id(1)
    nrow = 2 * t_new * DA_HEADS

    @pl.when(p == 0)
    def _():
        m_scr[...] = jnp.full(m_scr.shape, -jnp.inf, F32)
        l_scr[...] = jnp.zeros(l_scr.shape, F32)
        a_scr[...] = jnp.zeros(a_scr.shape, F32)

    q = q_ref[0] * (DA_DK ** -0.5)
    lane = lax.broadcasted_iota(jnp.int32, q.shape, 1)
    qrows = jnp.concatenate([jnp.where(lane < DA_DK, q, 0.0), jnp.where(lane >= DA_DK, q, 0.0)],
                            axis=0).astype(BF16)
    slope = slope_ref[...]

    def accumulate(scores, values):
        m_old = m_scr[...]
        m_new = m_old
        for s in scores:
            m_new = jnp.maximum(m_new, jnp.max(s, axis=1, keepdims=True))
        alpha = jnp.exp(m_old - m_new)
        l_new = alpha * l_scr[...]
        acc = alpha * a_scr[...]
        for s, vb in zip(scores, values):
            pr = jnp.exp(s - m_new)
            l_new = l_new + jnp.sum(pr, axis=1, keepdims=True)
            acc = acc + _dot(pr.astype(BF16), vb)
        m_scr[...] = m_new
        l_scr[...] = l_new
        a_scr[...] = acc

    cmat = cmat_ref[...]
    scores, values = [], []
    for i in range(pps):
        page_start = ((p * pps + i) * PAGE_SIZE).astype(F32)
        s = _dot_nt(qrows, kp_refs[i][0, 0].astype(BF16))
        scores.append(s + (cmat + slope * page_start))
        values.append(vp_refs[i][0, 0].astype(BF16))
    accumulate(scores, values)

    @pl.when(p == n_pages // pps - 1)
    def _():
        ncol = t_new * DA_HEADS
        r = lax.broadcasted_iota(jnp.int32, (nrow, ncol), 0)
        c = lax.broadcasted_iota(jnp.int32, (nrow, ncol), 1)
        hshift = DA_HEADS.bit_length() - 1
        dist = jnp.bitwise_and(jnp.right_shift(r, hshift), t_new - 1) - jnp.right_shift(c, hshift)
        ok = (jnp.bitwise_and(r, DA_HEADS - 1) == jnp.bitwise_and(c, DA_HEADS - 1)) & (dist >= 0)
        s = _dot_nt(qrows, kn_ref[0].astype(BF16))
        s = jnp.where(ok, s - slope * dist.astype(F32), NEG_BIG)
        accumulate([s], [vn_ref[0].astype(BF16)])
        on = a_scr[...] / l_scr[...]
        hr = nrow // 2
        o = on[:hr] - lam_ref[...] * on[hr:]
        o = o * lax.rsqrt(jnp.mean(o * o, axis=-1, keepdims=True) + LN_EPS) * g_ref[...]
        o_ref[0] = o * (1.0 - lam_init)


def _sattn(zq, zk, zv, cache_k, cache_v, page_table, layer, slopes, lam, subln_g, lam_init):
    n_seq, n_pages = page_table.shape
    pps = math.gcd(n_pages, MAX_PAGES_PER_STEP)
    hr = zq.shape[1]
    t_new = hr // DA_HEADS
    nrow = 2 * hr
    past = n_pages * PAGE_SIZE
    slope_col = jnp.tile(slopes, 2 * t_new).reshape(nrow, 1)
    r = jnp.arange(nrow)[:, None]
    c = jnp.arange(PAGE_SIZE * DA_HEADS)[None, :]
    dist0 = (past + (r // DA_HEADS) % t_new - c // DA_HEADS).astype(F32)
    cmat = jnp.where(r % DA_HEADS == c % DA_HEADS, -slope_col * dist0, NEG_BIG).astype(F32)
    seq_spec = pl.BlockSpec((1, hr, DA_DV), lambda s, p, pt: (s, 0, 0))
    const = lambda shape: pl.BlockSpec(shape, lambda s, p, pt: (0,) * len(shape))

    def page_spec(i):
        return pl.BlockSpec((1, 1, PAGE_SIZE * DA_HEADS, DA_DV),
                            lambda s, p, pt: (layer, pt[s * n_pages + p * pps + i], 0, 0))

    grid_spec = pltpu.PrefetchScalarGridSpec(
        num_scalar_prefetch=1,
        grid=(n_seq, n_pages // pps),
        in_specs=[seq_spec, seq_spec, seq_spec, const(cmat.shape), const((nrow, 1)), const((1, 1)),
                  const((1, DA_DV))] + [page_spec(i) for i in range(pps)] * 2,
        out_specs=seq_spec,
        scratch_shapes=[pltpu.VMEM((nrow, 1), F32), pltpu.VMEM((nrow, 1), F32),
                        pltpu.VMEM((nrow, DA_DV), F32)],
    )
    return pl.pallas_call(
        functools.partial(_sattn_kernel, n_pages=n_pages, pps=pps, t_new=t_new, lam_init=lam_init),
        grid_spec=grid_spec,
        out_shape=jax.ShapeDtypeStruct(zq.shape, F32),
        compiler_params=_cparams("arbitrary", "arbitrary"),
    )(page_table.reshape(-1), zq, zk, zv, cmat, slope_col, lam.reshape(1, 1),
      subln_g.reshape(1, DA_DV), *([cache_k] * pps), *([cache_v] * pps))


def _split_bf16(a):
    hi = a.astype(BF16)
    lo = (a - hi.astype(F32)).astype(BF16)
    return hi, lo


def _route(x, w, b):
    xh, xl = _split_bf16(x)
    wh, wl = _split_bf16(w)
    logits = _dot(xh, wh) + (_dot(xl, wh) + _dot(xh, wl)) + b
    lane = lax.broadcasted_iota(jnp.int32, logits.shape, 1).astype(F32)
    big = float(LANES)
    is_grp = lane < MOE_GROUPS
    gl = jnp.where(is_grp, logits, -jnp.inf)
    gmax = jnp.max(gl, axis=1, keepdims=True)
    gidx = jnp.min(jnp.where(gl == gmax, lane, big), axis=1, keepdims=True)
    den = jnp.sum(jnp.where(is_grp, jnp.exp(logits - gmax), 0.0), axis=1, keepdims=True)
    p_g = 1.0 / den
    lo = ROUTE_OFF + MOE_PER_GROUP * gidx
    in_grp = (lane >= lo) & (lane < lo + MOE_PER_GROUP)
    el = jnp.where(in_grp, logits, -jnp.inf)
    v1 = jnp.max(el, axis=1, keepdims=True)
    i1 = jnp.min(jnp.where(el == v1, lane, big), axis=1, keepdims=True)
    el2 = jnp.where(lane == i1, -jnp.inf, el)
    v2 = jnp.max(el2, axis=1, keepdims=True)
    i2 = jnp.min(jnp.where(el2 == v2, lane, big), axis=1, keepdims=True)
    e2 = jnp.exp(v2 - v1)
    w1 = (1.0 / (1.0 + e2)) * p_g
    w2 = (e2 / (1.0 + e2)) * p_g
    gates = jnp.where(lane == i1, w1, 0.0) + jnp.where(lane == i2, w2, 0.0)
    return gates, gidx, i1, i2


def _router_kernel(x_ref, w_ref, b_ref, g_ref):
    g_ref[...] = _route(x_ref[...], w_ref[...], b_ref[...])[0]


def _router(x, w_route, b_route, tm):
    m, d = x.shape
    tm = min(tm, m)
    return pl.pallas_call(
        _router_kernel,
        grid=(m // tm,),
        in_specs=[pl.BlockSpec((tm, d), lambda i: (i, 0)),
                  pl.BlockSpec((d, LANES), lambda i: (0, 0)),
                  pl.BlockSpec((1, LANES), lambda i: (0, 0))],
        out_specs=pl.BlockSpec((tm, LANES), lambda i: (i, 0)),
        out_shape=jax.ShapeDtypeStruct((m, LANES), F32),
        compiler_params=_cparams("arbitrary"),
    )(x, w_route, b_route)


def _moe_kernel(x_ref, g_ref, wg_ref, wu_ref, wd_ref, lg_ref, lb_ref, o_ref, xb_scr, y_scr):
    e = pl.program_id(1)

    @pl.when(e == 0)
    def _():
        xb_scr[...] = x_ref[...].astype(BF16)
        y_scr[...] = jnp.zeros(y_scr.shape, F32)

    lane = lax.broadcasted_iota(jnp.int32, g_ref.shape, 1)
    ge = jnp.sum(jnp.where(lane == e + ROUTE_OFF, g_ref[...], 0.0), axis=1, keepdims=True)
    xb = xb_scr[...]
    h = _silu(_dot(xb, wg_ref[0])) * _dot(xb, wu_ref[0]) * ge
    y_scr[...] += _dot(h.astype(BF16), wd_ref[0])

    @pl.when(e == MOE_EXPERTS - 1)
    def _():
        o_ref[...] = _layer_norm(DN_ALPHA * x_ref[...] + y_scr[...], lg_ref[...], lb_ref[...])


def _moe_ln(x, gates, wg, wu, wd, g, b, tm):
    m, d = x.shape
    tm = min(tm, m)
    return pl.pallas_call(
        _moe_kernel,
        grid=(m // tm, MOE_EXPERTS),
        in_specs=[pl.BlockSpec((tm, d), lambda i, e: (i, 0)),
                  pl.BlockSpec((tm, LANES), lambda i, e: (i, 0)),
                  pl.BlockSpec((1, d, MOE_FF), lambda i, e: (e, 0, 0)),
                  pl.BlockSpec((1, d, MOE_FF), lambda i, e: (e, 0, 0)),
                  pl.BlockSpec((1, MOE_FF, d), lambda i, e: (e, 0, 0)),
                  pl.BlockSpec((1, d), lambda i, e: (0, 0)),
                  pl.BlockSpec((1, d), lambda i, e: (0, 0))],
        out_specs=pl.BlockSpec((tm, d), lambda i, e: (i, 0)),
        out_shape=jax.ShapeDtypeStruct((m, d), F32),
        scratch_shapes=[pltpu.VMEM((tm, d), BF16), pltpu.VMEM((tm, d), F32)],
        compiler_params=_cparams("arbitrary", "arbitrary"),
    )(x, gates, wg, wu, wd, g.reshape(1, d), b.reshape(1, d))


MOE_PAIRS = [(a, b) for a in range(MOE_PER_GROUP) for b in range(a + 1, MOE_PER_GROUP)]
MOE_CLASSES = MOE_GROUPS * len(MOE_PAIRS)
CLASS_ELO = [g * MOE_PER_GROUP + a for g in range(MOE_GROUPS) for a, _ in MOE_PAIRS]
CLASS_EHI = [g * MOE_PER_GROUP + b for g in range(MOE_GROUPS) for _, b in MOE_PAIRS]
MOE_TILE = 256
PERM_CHUNK = 2048


def _router_sort_kernel(x_ref, w_ref, b_ref, cls_ref, rank_ref, cnt_ref, run_scr, *, n_steps):
    i = pl.program_id(0)

    @pl.when(i == 0)
    def _():
        run_scr[...] = jnp.zeros(run_scr.shape, F32)

    _, gidx, i1, i2 = _route(x_ref[...], w_ref[...], b_ref[...])
    base = ROUTE_OFF + MOE_PER_GROUP * gidx
    lo = jnp.minimum(i1, i2) - base
    hi = jnp.maximum(i1, i2) - base
    pair = jnp.where(lo == 0.0, hi - 1.0, jnp.where(lo == 1.0, hi + 1.0, float(len(MOE_PAIRS) - 1)))
    cls = gidx * float(len(MOE_PAIRS)) + pair
    tm = cls.shape[0]
    lane = lax.broadcasted_iota(jnp.int32, (tm, LANES), 1).astype(F32)
    onehot = jnp.where(lane == cls, 1.0, 0.0)
    r = lax.broadcasted_iota(jnp.int32, (tm, tm), 0)
    c = lax.broadcasted_iota(jnp.int32, (tm, tm), 1)
    earlier = jnp.where(c < r, 1.0, 0.0).astype(BF16)
    within = _dot(earlier, onehot.astype(BF16))
    rank = jnp.sum(onehot * (within + run_scr[...]), axis=1, keepdims=True)
    run_scr[...] += jnp.sum(onehot, axis=0, keepdims=True)
    cls_ref[...] = cls.astype(jnp.int32)
    rank_ref[...] = rank.astype(jnp.int32)

    @pl.when(i == n_steps - 1)
    def _():
        cnt_ref[...] = run_scr[...]


def _router_sort(x, w_route, b_route, tm):
    m, d = x.shape
    tm = min(tm, m)
    col = pl.BlockSpec((tm, 1), lambda i: (i, 0))
    return pl.pallas_call(
        functools.partial(_router_sort_kernel, n_steps=m // tm),
        grid=(m // tm,),
        in_specs=[pl.BlockSpec((tm, d), lambda i: (i, 0)),
                  pl.BlockSpec((d, LANES), lambda i: (0, 0)),
                  pl.BlockSpec((1, LANES), lambda i: (0, 0))],
        out_specs=[col, col, pl.BlockSpec((1, LANES), lambda i: (0, 0))],
        out_shape=[jax.ShapeDtypeStruct((m, 1), jnp.int32), jax.ShapeDtypeStruct((m, 1), jnp.int32),
                   jax.ShapeDtypeStruct((1, LANES), F32)],
        scratch_shapes=[pltpu.VMEM((1, LANES), F32)],
        compiler_params=_cparams("arbitrary"),
    )(x, w_route, b_route)


ROW_SLABS = D_MODEL // LANES


def _to_slabs(rows, slab_ref):
    for j in range(ROW_SLABS):
        slab_ref[:, j, :] = rows[:, j * LANES:(j + 1) * LANES]


def _from_slabs(slab_ref):
    return jnp.concatenate([slab_ref[:, j, :] for j in range(ROW_SLABS)], axis=1)


PERM_SUB = 4


def _for_rows(fn, lo, hi):
    def body(r, carry):
        fn(r)
        return carry

    lax.fori_loop(lo, hi, body, 0, unroll=8)


def _scatter_kernel(idx_ref, x_ref, dst_in_ref, dst_ref, slab_scr, sems, *, chunk):
    del dst_in_ref
    sub = chunk // PERM_SUB

    def row_copy(r, k):
        return pltpu.make_async_copy(slab_scr.at[r], dst_ref.at[idx_ref[0, 0, r]], sems.at[k])

    for k in range(PERM_SUB):
        rows = slice(k * sub, (k + 1) * sub)
        for j in range(ROW_SLABS):
            slab_scr[rows, j, :] = x_ref[rows, j * LANES:(j + 1) * LANES]
        _for_rows(lambda r, k=k: row_copy(r, k).start(), k * sub, (k + 1) * sub)
    for k in range(PERM_SUB):
        _for_rows(lambda r, k=k: row_copy(r, k).wait(), k * sub, (k + 1) * sub)


def _gather_kernel(idx_ref, src_ref, o_ref, slab_scr, sems, *, chunk):
    sub = chunk // PERM_SUB

    def row_copy(r, k):
        return pltpu.make_async_copy(src_ref.at[idx_ref[0, 0, r]], slab_scr.at[r], sems.at[k])

    for k in range(PERM_SUB):
        _for_rows(lambda r, k=k: row_copy(r, k).start(), k * sub, (k + 1) * sub)
    for k in range(PERM_SUB):
        rows = slice(k * sub, (k + 1) * sub)
        _for_rows(lambda r, k=k: row_copy(r, k).wait(), k * sub, (k + 1) * sub)
        o_ref[rows, :] = jnp.concatenate([slab_scr[rows, j, :] for j in range(ROW_SLABS)], axis=1)


def _permute_rows(src, idx, dst_init, gather):
    n = idx.shape[0]
    chunk = min(PERM_CHUNK, n)
    any_spec = pl.BlockSpec(memory_space=pl.ANY)
    idx_spec = pl.BlockSpec((1, 1, chunk), lambda i: (i, 0, 0), memory_space=pltpu.SMEM)
    row_spec = pl.BlockSpec((chunk, D_MODEL), lambda i: (i, 0))
    scratch = [pltpu.VMEM((chunk, ROW_SLABS, LANES), F32), pltpu.SemaphoreType.DMA((PERM_SUB,))]
    params = pltpu.CompilerParams(dimension_semantics=("arbitrary",), vmem_limit_bytes=VMEM_LIMIT,
                                  disable_bounds_checks=True)
    idx3 = idx.reshape(n // chunk, 1, chunk)
    if gather:
        return pl.pallas_call(
            functools.partial(_gather_kernel, chunk=chunk),
            grid=(n // chunk,),
            in_specs=[idx_spec, any_spec],
            out_specs=row_spec,
            out_shape=jax.ShapeDtypeStruct((n, D_MODEL), F32),
            scratch_shapes=scratch,
            compiler_params=params,
        )(idx3, src)
    return pl.pallas_call(
        functools.partial(_scatter_kernel, chunk=chunk),
        grid=(n // chunk,),
        in_specs=[idx_spec, row_spec, any_spec],
        out_specs=any_spec,
        out_shape=jax.ShapeDtypeStruct(dst_init.shape, F32),
        scratch_shapes=scratch,
        input_output_aliases={2: 0},
        compiler_params=params,
    )(idx3, src, dst_init)


def _moe_tile_kernel(elo_ref, ehi_ref, ok_ref, x_ref, wr_ref, br_ref,
                     wg0, wu0, wd0, wg1, wu1, wd1, lg_ref, lb_ref, o_ref):
    t = pl.program_id(0)

    @pl.when(ok_ref[t] == 0)
    def _():
        o_ref[...] = jnp.zeros(o_ref.shape, F32)

    @pl.when(ok_ref[t] != 0)
    def _():
        x = _from_slabs(x_ref)
        gates = _route(x, wr_ref[...], br_ref[...])[0]
        lane = lax.broadcasted_iota(jnp.int32, gates.shape, 1)
        xb = x.astype(BF16)
        y = None
        for e, wg, wu, wd in ((elo_ref[t], wg0, wu0, wd0), (ehi_ref[t], wg1, wu1, wd1)):
            ge = jnp.sum(jnp.where(lane == e + ROUTE_OFF, gates, 0.0), axis=1, keepdims=True)
            h = _silu(_dot(xb, wg[0])) * _dot(xb, wu[0]) * ge
            ye = _dot(h.astype(BF16), wd[0])
            y = ye if y is None else y + ye
        _to_slabs(_layer_norm(DN_ALPHA * x + y, lg_ref[...], lb_ref[...]), o_ref)


def _moe_sorted_ln(x, w_route, b_route, wg, wu, wd, g, b, tm_router):
    n, d = x.shape
    tile = MOE_TILE
    n_tiles = n // tile + MOE_CLASSES
    cls, rank, cnt = _router_sort(x, w_route, b_route, tm_router)
    counts = cnt[0, :MOE_CLASSES].astype(jnp.int32)
    tiles_c = (counts + tile - 1) // tile
    tile_end = jnp.cumsum(tiles_c)
    tile_start = tile_end - tiles_c
    dest = jnp.take(tile_start, cls[:, 0]) * tile + rank[:, 0]
    tids = jnp.arange(n_tiles, dtype=jnp.int32)
    n_used = tile_end[-1]
    t_cls = jnp.searchsorted(tile_end, jnp.minimum(tids, n_used - 1), side='right').astype(jnp.int32)
    t_elo = jnp.take(jnp.array(CLASS_ELO, jnp.int32), t_cls)
    t_ehi = jnp.take(jnp.array(CLASS_EHI, jnp.int32), t_cls)
    t_ok = (tids < n_used).astype(jnp.int32)

    xs = _permute_rows(x, dest, jnp.zeros((n_tiles * tile, ROW_SLABS, LANES), F32), gather=False)
    wspec = lambda tab_pos: pl.BlockSpec(
        (1,) + wg.shape[1:], lambda t, elo, ehi, ok: ((elo, ehi)[tab_pos][t], 0, 0))
    dspec = lambda tab_pos: pl.BlockSpec(
        (1,) + wd.shape[1:], lambda t, elo, ehi, ok: ((elo, ehi)[tab_pos][t], 0, 0))
    const = lambda shape: pl.BlockSpec(shape, lambda t, elo, ehi, ok: (0,) * len(shape))
    row_spec = pl.BlockSpec((tile, ROW_SLABS, LANES), lambda t, elo, ehi, ok: (t, 0, 0))
    grid_spec = pltpu.PrefetchScalarGridSpec(
        num_scalar_prefetch=3,
        grid=(n_tiles,),
        in_specs=[row_spec, const(w_route.shape), const(b_route.shape),
                  wspec(0), wspec(0), dspec(0), wspec(1), wspec(1), dspec(1),
                  const((1, d)), const((1, d))],
        out_specs=row_spec,
    )
    ys = pl.pallas_call(
        _moe_tile_kernel,
        grid_spec=grid_spec,
        out_shape=jax.ShapeDtypeStruct(xs.shape, F32),
        compiler_params=_cparams("arbitrary"),
    )(t_elo, t_ehi, t_ok, xs, w_route, b_route, wg, wu, wd, wg, wu, wd, g.reshape(1, d), b.reshape(1, d))
    return _permute_rows(ys, dest, None, gather=True)


def _pad_time(a, n_seq, t_len, t_pad):
    w = a.shape[1]
    a = a.reshape(n_seq, t_len, w)
    a = jnp.pad(a, ((0, 0), (0, t_pad - t_len), (0, 0)))
    return a.reshape(n_seq * t_pad, w)


def _forward(x_prompt, x_sample, cache_k, cache_v, state_hgrn, state_s5_re, state_s5_im, page_table,
             w_in_ab, w_out_ab, hg_lb_param, hg_norm_g, s5_lam_re, s5_lam_im, s5_log_step,
             s5_b_re, s5_b_im, s5_c_re, s5_c_im, s5_d, s5_w_glu,
             w_in_c, w_out_c, da_lam_q1, da_lam_k1, da_lam_q2, da_lam_k2, da_subln_g,
             moe_w_group, moe_b_group, moe_w_expert, moe_b_expert, moe_w_gate, moe_w_up, moe_w_down,
             ln_g, ln_b):
    bp, tp, d = x_prompt.shape
    bs, ts, _ = x_sample.shape
    ts_pad = SUBLANES
    n_pool = cache_k.shape[1]
    xp = x_prompt.reshape(bp * tp, d)
    xs = x_sample.reshape(bs * ts, d)
    tm_p = min(512, bp * tp)
    tq = min(512, tp)
    s5_tb = min(256, tp)

    slopes = 2.0 ** (-8.0 * jnp.arange(1, DA_HEADS + 1, dtype=F32) / DA_HEADS)
    sm = jax.nn.softmax(hg_lb_param.astype(F32), axis=0)
    lbs = jnp.cumsum(sm, axis=0) - sm[:1]
    ck = cache_k.reshape(cache_k.shape[0], n_pool, PAGE_SIZE * DA_HEADS, 2 * DA_DK)
    cv = cache_v.reshape(cache_v.shape[0], n_pool, PAGE_SIZE * DA_HEADS, DA_DV)
    zeros_hg = jnp.zeros((bp, HG_HEADS, HG_DK, HG_DK), F32)
    zeros_s5 = jnp.zeros((bp, 1, S5_N), F32)

    new_hg_p, new_re_p, new_im_p = [], [], []
    kv_bufs_p = None
    new_k_s, new_v_s, new_hg_s, new_re_s, new_im_s = [], [], [], [], []
    for i in range(DEPTH):
        j = i // 2
        if i % 2 == 0:
            w_in = w_in_ab[j].astype(BF16)
            w_out = w_out_ab[j].astype(BF16)
            s5w = _s5_weights(s5_lam_re[j], s5_lam_im[j], s5_log_step[j], s5_b_re[j], s5_b_im[j],
                              s5_c_re[j], s5_c_im[j], s5_d[j], s5_w_glu[j])
            zp = _matmul(xp, w_in, tm_p)
            oa_p, hg_p = _hgrn(zp, lbs[j], hg_norm_g[j], zeros_hg, bp, tp, math.gcd(tp, HG_CHUNK), tp)
            ob_p, re_p, im_p = _s5(zp, s5w, zeros_s5, zeros_s5, bp, tp, s5_tb, False, SUBLANES)
            xp = _proj_ln([oa_p, ob_p], [w_out[:HG_WIDTH], w_out[HG_WIDTH:]], xp,
                          ln_g[i, 0], ln_b[i, 0], tm_p)
            zs = _pad_time(_matmul(xs, w_in, bs * ts), bs, ts, ts_pad)
            s0t = jnp.swapaxes(state_hgrn[j], -1, -2)
            oa_s, hg_s = _hgrn(zs, lbs[j], hg_norm_g[j], s0t, bs, ts_pad, ts_pad, ts)
            ob_s, re_s, im_s = _s5(zs, s5w, state_s5_re[j].reshape(bs, 1, S5_N),
                                   state_s5_im[j].reshape(bs, 1, S5_N), bs, ts_pad, ts_pad, True, ts)
            unpad = lambda a: a.reshape(bs, ts_pad, -1)[:, :ts].reshape(bs * ts, -1)
            xs = _proj_ln([unpad(oa_s), unpad(ob_s)], [w_out[:HG_WIDTH], w_out[HG_WIDTH:]], xs,
                          ln_g[i, 0], ln_b[i, 0], bs * ts)
            new_hg_p.append(jnp.swapaxes(hg_p, -1, -2))
            new_hg_s.append(jnp.swapaxes(hg_s, -1, -2))
            new_re_p.append(re_p.reshape(bp, S5_GROUPS, S5_STATE))
            new_im_p.append(im_p.reshape(bp, S5_GROUPS, S5_STATE))
            new_re_s.append(re_s.reshape(bs, S5_GROUPS, S5_STATE))
            new_im_s.append(im_s.reshape(bs, S5_GROUPS, S5_STATE))
        else:
            w_in = w_in_c[j].astype(BF16)
            w_out = w_out_c[j].astype(BF16)
            lam_init = 0.8 - 0.6 * math.exp(-0.3 * i)
            lam = (jnp.exp(jnp.sum(da_lam_q1[j].astype(F32) * da_lam_k1[j].astype(F32)))
                   - jnp.exp(jnp.sum(da_lam_q2[j].astype(F32) * da_lam_k2[j].astype(F32))) + lam_init)
            qp, kbuf_p, vbuf_p, vt_p = _qkv_proj(xp, w_in, kv_bufs_p, j, DEPTH // 2, tm_p // 2)
            kv_bufs_p = (kbuf_p, vbuf_p)
            oc_p = _pattn(qp, kbuf_p, vt_p, j, slopes, lam, da_subln_g[j], bp, tp, tq, lam_init)
            xp = _proj_ln([oc_p], [w_out], xp, ln_g[i, 0], ln_b[i, 0], tm_p)

            zs = _matmul(xs, w_in, bs * ts)
            rows = lambda a: a.reshape(bs, ts * DA_HEADS, DA_DV)
            zq_s, zk_s, zv_s = zs[:, :DA_QK], zs[:, DA_QK:2 * DA_QK], zs[:, 2 * DA_QK:]
            oc_s = _sattn(rows(zq_s), rows(zk_s), rows(zv_s), ck, cv, page_table, j,
                          slopes, lam, da_subln_g[j], lam_init)
            new_k_s.append(zk_s.reshape(bs, ts, DA_HEADS, 2 * DA_DK))
            new_v_s.append(zv_s.reshape(bs, ts, DA_HEADS, DA_DV))
            xs = _proj_ln([oc_s.reshape(bs * ts, DA_HEADS * DA_DV)], [w_out], xs,
                          ln_g[i, 0], ln_b[i, 0], bs * ts)

        w_route = jnp.pad(jnp.concatenate([moe_w_group[i], moe_w_expert[i]], axis=1).astype(F32),
                          ((0, 0), (0, LANES - MOE_GROUPS - MOE_EXPERTS)))
        b_route = jnp.pad(jnp.concatenate([moe_b_group[i], moe_b_expert[i]]).astype(F32),
                          (0, LANES - MOE_GROUPS - MOE_EXPERTS)).reshape(1, LANES)
        wg = moe_w_gate[i].astype(BF16)
        wu = moe_w_up[i].astype(BF16)
        wd = moe_w_down[i].astype(BF16)
        xp = _moe_sorted_ln(xp, w_route, b_route, wg, wu, wd, ln_g[i, 1], ln_b[i, 1], tm_p)
        xs = _moe_ln(xs, _router(xs, w_route, b_route, bs * ts), wg, wu, wd,
                     ln_g[i, 1], ln_b[i, 1], bs * ts)

    return (xp.reshape(bp, tp, d), xs.reshape(bs, ts, d),
            kv_bufs_p[0].reshape(DEPTH // 2, bp, tp, DA_HEADS, 2 * DA_DK),
            kv_bufs_p[1].reshape(DEPTH // 2, bp, tp, DA_HEADS, DA_DV), jnp.stack(new_hg_p),
            jnp.stack(new_re_p), jnp.stack(new_im_p),
            jnp.stack(new_k_s), jnp.stack(new_v_s), jnp.stack(new_hg_s),
            jnp.stack(new_re_s), jnp.stack(new_im_s))


_forward_jit = jax.jit(_forward)


def kernel(x_prompt, x_sample, cache_k, cache_v, state_hgrn, state_s5_re, state_s5_im, page_table, w_in_ab, w_out_ab, hg_lb_param, hg_norm_g, s5_lam_re, s5_lam_im, s5_log_step, s5_b_re, s5_b_im, s5_c_re, s5_c_im, s5_d, s5_w_glu, w_in_c, w_out_c, da_lam_q1, da_lam_k1, da_lam_q2, da_lam_k2, da_subln_g, moe_w_group, moe_b_group, moe_w_expert, moe_b_expert, moe_w_gate, moe_w_up, moe_w_down, ln_g, ln_b):
    return _forward_jit(x_prompt, x_sample, cache_k, cache_v, state_hgrn, state_s5_re, state_s5_im, page_table, w_in_ab, w_out_ab, hg_lb_param, hg_norm_g, s5_lam_re, s5_lam_im, s5_log_step, s5_b_re, s5_b_im, s5_c_re, s5_c_im, s5_d, s5_w_glu, w_in_c, w_out_c, da_lam_q1, da_lam_k1, da_lam_q2, da_lam_k2, da_subln_g, moe_w_group, moe_b_group, moe_w_expert, moe_b_expert, moe_w_gate, moe_w_up, moe_w_down, ln_g, ln_b)
```
